```python
import math
import jax, jax.numpy as jnp
from jax import lax
import numpy as np

D_MODEL = 1024
BATCH = 4
SEQ = 8192
DEPTH = 2

N_A = DEPTH // 2
N_B = DEPTH - N_A
CHUNK = 128
D_GATE = D_MODEL
N_GROUPS = 8
GROUP_DIM = D_GATE // N_GROUPS
N_HEADS = 8
HEAD_DIM = 64
V_DIM = 2 * HEAD_DIM
QK_WIDTH = N_HEADS * 2 * HEAD_DIM
V_WIDTH = N_HEADS * V_DIM
Q_BLOCK = 128
D_FF = int(math.ceil(8 * D_MODEL / 3 / 256) * 256)
EPS = 1e-6

kernel_name = "yoco_gmlp_diffattn_alibi_swiglu"


def _alibi_slopes(n):
    return np.array([2.0 ** (-8.0 * (i + 1) / n) for i in range(n)], dtype=np.float32)


def _lambda_init(layer_idx):
    return 0.8 - 0.6 * math.exp(-0.3 * layer_idx)


def rms_norm(x, g):
    xf = x.astype(jnp.float32)
    y = xf * lax.rsqrt(jnp.mean(xf * xf, axis=-1, keepdims=True) + EPS)
    return (y * g.astype(jnp.float32)).astype(x.dtype)


def layer_norm_gain(x, g):
    xf = x.astype(jnp.float32)
    mu = jnp.mean(xf, axis=-1, keepdims=True)
    var = jnp.mean(jnp.square(xf - mu), axis=-1, keepdims=True)
    return ((xf - mu) * lax.rsqrt(var + EPS) * g.astype(jnp.float32)).astype(x.dtype)


def swiglu(h, w_gu, w_down):
    gu = h @ w_gu
    g, u = jnp.split(gu, 2, axis=-1)
    return (jax.nn.silu(g) * u) @ w_down


def gmlp_mixer(h, w_in, v_gain, w_sp, b_sp, w_out):
    B, S, _ = h.shape
    uv = jax.nn.gelu(h @ w_in)
    u, v = jnp.split(uv, 2, axis=-1)
    v = layer_norm_gain(v, v_gain)
    v = v.reshape(B, S // CHUNK, CHUNK, N_GROUPS, GROUP_DIM)
    causal = jnp.tril(jnp.ones((CHUNK, CHUNK), dtype=w_sp.dtype))
    z = jnp.einsum('gts,bcsgd->bctgd', w_sp * causal, v)
    z = z + b_sp.T[None, None, :, :, None]
    return (u * z.reshape(B, S, D_GATE)) @ w_out


def shared_kv(x, g, w_kv):
    B, S, _ = x.shape
    kv = rms_norm(x, g) @ w_kv
    k = kv[..., :QK_WIDTH].reshape(B, S, N_HEADS, 2, HEAD_DIM).transpose(0, 2, 3, 1, 4)
    v = kv[..., QK_WIDTH:].reshape(B, S, N_HEADS, V_DIM).transpose(0, 2, 1, 3)
    return k, v


def diff_attention(h, k, v, w_q, lam_vecs, subln, w_o, lambda_init):
    B, S, _ = h.shape
    n_blk = S // Q_BLOCK
    q = (h @ w_q).reshape(B, S, N_HEADS, 2, HEAD_DIM)
    q = q.reshape(B, n_blk, Q_BLOCK, N_HEADS, 2, HEAD_DIM).transpose(1, 0, 3, 4, 2, 5)
    lf = lam_vecs.astype(jnp.float32)
    lam = jnp.exp(jnp.sum(lf[0] * lf[1])) - jnp.exp(jnp.sum(lf[2] * lf[3])) + lambda_init
    slopes = jnp.asarray(_alibi_slopes(N_HEADS))
    scale = HEAD_DIM ** -0.5
    kf = k.astype(jnp.float32)
    vf = v.astype(jnp.float32)
    key_pos = jnp.arange(S, dtype=jnp.int32)

    def block(args):
        qb, bi = args
        q_pos = bi * Q_BLOCK + jnp.arange(Q_BLOCK, dtype=jnp.int32)
        dist = (q_pos[:, None] - key_pos[None, :]).astype(jnp.float32)
        bias = -slopes[:, None, None] * dist[None]
        s = jnp.einsum('bhitd,bhisd->bhits', qb.astype(jnp.float32), kf) * scale
        s = s + bias[None, :, None]
        s = jnp.where((dist >= 0)[None, None, None], s, -jnp.inf)
        p = jax.nn.softmax(s, axis=-1)
        a = p[:, :, 0] - lam * p[:, :, 1]
        return jnp.einsum('bhts,bhsd->bhtd', a, vf)

    o = lax.map(block, (q, jnp.arange(n_blk, dtype=jnp.int32)))
    o = o.transpose(1, 0, 3, 2, 4).reshape(B, S, N_HEADS, V_DIM).astype(h.dtype)
    o = rms_norm(o, subln) * (1.0 - lambda_init)
    return o.reshape(B, S, V_WIDTH) @ w_o


def setup_inputs(seed: int = 0) -> dict:
    key = jax.random.key(seed)
    ks = jax.random.split(key, 20)
    f32 = jnp.float32

    def w(k, shape, fan_in):
        return jax.random.normal(k, shape, f32) * (fan_in ** -0.5)

    def gain(k, shape):
        return 1.0 + 0.05 * jax.random.normal(k, shape, f32)

    return {
        "x": jax.random.normal(ks[0], (BATCH, SEQ, D_MODEL), f32),
        "a_norm": gain(ks[1], (N_A, D_MODEL)),
        "a_w_in": w(ks[2], (N_A, D_MODEL, 2 * D_GATE), D_MODEL),
        "a_v_norm": gain(ks[3], (N_A, D_GATE)),
        "a_w_sp": w(ks[4], (N_A, N_GROUPS, CHUNK, CHUNK), CHUNK),
        "a_b_sp": 1.0 + 0.1 * jax.random.normal(ks[5], (N_A, N_GROUPS, CHUNK), f32),
        "a_w_out": w(ks[6], (N_A, D_GATE, D_MODEL), D_GATE),
        "ffn_norm": gain(ks[7], (DEPTH, D_MODEL)),
        "ffn_w_gu": w(ks[8], (DEPTH, D_MODEL, 2 * D_FF), D_MODEL),
        "ffn_w_down": w(ks[9], (DEPTH, D_FF, D_MODEL), D_FF),
        "kv_norm": gain(ks[10], (D_MODEL,)),
        "kv_w": w(ks[11], (D_MODEL, QK_WIDTH + V_WIDTH), D_MODEL),
        "b_norm": gain(ks[12], (N_B, D_MODEL)),
        "b_w_q": w(ks[13], (N_B, D_MODEL, QK_WIDTH), D_MODEL),
        "b_lambda": 0.1 * jax.random.normal(ks[14], (N_B, 4, HEAD_DIM), f32),
        "b_subln": gain(ks[15], (N_B, V_DIM)),
        "b_w_o": w(ks[16], (N_B, V_WIDTH, D_MODEL), V_WIDTH),
        "final_norm": gain(ks[17], (D_MODEL,)),
    }


def reference(x, a_norm, a_w_in, a_v_norm, a_w_sp, a_b_sp, a_w_out, ffn_norm, ffn_w_gu, ffn_w_down,
              kv_norm, kv_w, b_norm, b_w_q, b_lambda, b_subln, b_w_o, final_norm):
    k = v = None
    for l in range(DEPTH):
        if l < N_A:
            x = x + gmlp_mixer(rms_norm(x, a_norm[l]), a_w_in[l], a_v_norm[l],
                               a_w_sp[l], a_b_sp[l], a_w_out[l])
        else:
            j = l - N_A
            if j == 0:
                k, v = shared_kv(x, kv_norm, kv_w)
            x = x + diff_attention(rms_norm(x, b_norm[j]), k, v, b_w_q[j], b_lambda[j],
                                   b_subln[j], b_w_o[j], _lambda_init(l))
        x = x + swiglu(rms_norm(x, ffn_norm[l]), ffn_w_gu[l], ffn_w_down[l])
    return rms_norm(x, final_norm)
```

```python
import functools
import math

import jax
import jax.numpy as jnp
import numpy as np
from jax import lax
from jax.experimental import pallas as pl
from jax.experimental.pallas import tpu as pltpu

D_MODEL = 1024
CHUNK = 128
N_GROUPS = 8
GROUP_DIM = D_MODEL // N_GROUPS
N_HEADS = 8
HEAD_DIM = 64
V_DIM = 2 * HEAD_DIM
D_FF = 2816
EPS = 1e-6
N_A = 1
LAMBDA_INIT = 0.8 - 0.6 * math.exp(-0.3 * N_A)

V7X_VMEM_LIMIT_BYTES = 56 * 1024 * 1024

ROW_TILE = 256
Q_TILE = 256
K_TILE = 256

F32 = jnp.float32
BF16 = jnp.bfloat16


def _rms_scale(x):
    return lax.rsqrt(jnp.mean(x * x, axis=-1, keepdims=True) + EPS)


def _gelu_tanh(x):
    c = math.sqrt(2.0 / math.pi)
    return 0.5 * x * (1.0 + jnp.tanh(c * (x + 0.044715 * (x * x * x))))


def _silu(x):
    return 0.5 * x * (1.0 + jnp.tanh(0.5 * x))


def _const_spec(shape):
    nd = len(shape)
    return pl.BlockSpec(shape, lambda *_: (0,) * nd, pipeline_mode=pl.Buffered(1))


def _row_spec(width):
    return pl.BlockSpec((ROW_TILE, width), lambda i: (i, 0))


def _dense_params():
    return pltpu.CompilerParams(dimension_semantics=("arbitrary",),
                                vmem_limit_bytes=V7X_VMEM_LIMIT_BYTES)


def _gmlp_kernel(x_ref, an_ref, win_ref, vg_ref, wsp_ref, bsp_ref, wout_ref, o_ref):
    x = x_ref[...]
    h = (x * _rms_scale(x) * an_ref[...]).astype(BF16)
    uv = _gelu_tanh(jnp.dot(h, win_ref[...], preferred_element_type=F32))
    u = uv[:, :D_MODEL]
    v = uv[:, D_MODEL:]
    mu = jnp.mean(v, axis=-1, keepdims=True)
    vc = v - mu
    var = jnp.mean(vc * vc, axis=-1, keepdims=True)
    vn = (vc * lax.rsqrt(var + EPS) * vg_ref[...]).astype(BF16)

    t_idx = lax.broadcasted_iota(jnp.int32, (CHUNK, CHUNK), 0)
    s_idx = lax.broadcasted_iota(jnp.int32, (CHUNK, CHUNK), 1)
    causal = s_idx <= t_idx
    n_chunks = ROW_TILE // CHUNK
    cols = []
    for g in range(N_GROUPS):
        wm = jnp.where(causal, wsp_ref[g], 0.0).astype(BF16)
        bias = bsp_ref[:, g * GROUP_DIM:(g + 1) * GROUP_DIM]
        rows = []
        for c in range(n_chunks):
            vgc = vn[c * CHUNK:(c + 1) * CHUNK, g * GROUP_DIM:(g + 1) * GROUP_DIM]
            rows.append(jnp.dot(wm, vgc, preferred_element_type=F32) + bias)
        cols.append(jnp.concatenate(rows, axis=0))
    z = jnp.concatenate(cols, axis=1)
    gated = (u * z).astype(BF16)
    o_ref[...] = x + jnp.dot(gated, wout_ref[...], preferred_element_type=F32)


def _gmlp_layer(x, a_norm, w_in, v_gain, w_sp, b_full, w_out):
    n = x.shape[0]
    return pl.pallas_call(
        _gmlp_kernel,
        grid=(n // ROW_TILE,),
        in_specs=[
            _row_spec(D_MODEL),
            _const_spec((1, D_MODEL)),
            _const_spec((D_MODEL, 2 * D_MODEL)),
            _const_spec((1, D_MODEL)),
            _const_spec((N_GROUPS, CHUNK, CHUNK)),
            _const_spec((CHUNK, D_MODEL)),
            _const_spec((D_MODEL, D_MODEL)),
        ],
        out_specs=_row_spec(D_MODEL),
        out_shape=jax.ShapeDtypeStruct((n, D_MODEL), F32),
        compiler_params=_dense_params(),
        name="gmlp_mixer",
    )(x, a_norm, w_in, v_gain, w_sp, b_full, w_out)


def _swiglu(x, fn, wgu_ref, wd_ref):
    h = (x * _rms_scale(x) * fn).astype(BF16)
    gu = jnp.dot(h, wgu_ref[...], preferred_element_type=F32)
    a = (_silu(gu[:, :D_FF]) * gu[:, D_FF:]).astype(BF16)
    return x + jnp.dot(a, wd_ref[...], preferred_element_type=F32)


def _ffn_qkv_kernel(x_ref, fn_ref, wgu_ref, wd_ref, kvn_ref, wkv_ref, qn_ref, wq_ref,
                    x_out, q_out, k_out, v_out):
    y = _swiglu(x_ref[...], fn_ref[...], wgu_ref, wd_ref)
    x_out[...] = y
    yn = y * _rms_scale(y)
    kv = jnp.dot((yn * kvn_ref[...]).astype(BF16), wkv_ref[...], preferred_element_type=F32)
    k_out[...] = kv[:, :D_MODEL].astype(BF16)
    v_out[...] = kv[:, D_MODEL:].astype(BF16)
    q = jnp.dot((yn * qn_ref[...]).astype(BF16), wq_ref[...], preferred_element_type=F32)
    q_out[...] = (q * (HEAD_DIM ** -0.5)).astype(BF16)


def _ffn_qkv_layer(x, fn, w_gu, w_down, kv_norm, w_kv, q_norm, w_q):
    n = x.shape[0]
    return pl.pallas_call(
        _ffn_qkv_kernel,
        grid=(n // ROW_TILE,),
        in_specs=[
            _row_spec(D_MODEL),
            _const_spec((1, D_MODEL)),
            _const_spec((D_MODEL, 2 * D_FF)),
            _const_spec((D_FF, D_MODEL)),
            _const_spec((1, D_MODEL)),
            _const_spec((D_MODEL, 2 * D_MODEL)),
            _const_spec((1, D_MODEL)),
            _const_spec((D_MODEL, D_MODEL)),
        ],
        out_specs=[_row_spec(D_MODEL)] * 4,
        out_shape=[jax.ShapeDtypeStruct((n, D_MODEL), F32)]
        + [jax.ShapeDtypeStruct((n, D_MODEL), BF16)] * 3,
        compiler_params=_dense_params(),
        name="ffn0_qkv",
    )(x, fn, w_gu, w_down, kv_norm, w_kv, q_norm, w_q)


def _oproj_ffn_final_kernel(x_ref, a_ref, wo_ref, fn_ref, wgu_ref, wd_ref, gn_ref, o_ref):
    x = x_ref[...] + jnp.dot(a_ref[...], wo_ref[...], preferred_element_type=F32)
    y = _swiglu(x, fn_ref[...], wgu_ref, wd_ref)
    o_ref[...] = y * _rms_scale(y) * gn_ref[...]


def _oproj_ffn_final_layer(x, attn, w_o, fn, w_gu, w_down, final_norm):
    n = x.shape[0]
    return pl.pallas_call(
        _oproj_ffn_final_kernel,
        grid=(n // ROW_TILE,),
        in_specs=[
            _row_spec(D_MODEL),
            _row_spec(D_MODEL),
            _const_spec((D_MODEL, D_MODEL)),
            _const_spec((1, D_MODEL)),
            _const_spec((D_MODEL, 2 * D_FF)),
            _const_spec((D_FF, D_MODEL)),
            _const_spec((1, D_MODEL)),
        ],
        out_specs=_row_spec(D_MODEL),
        out_shape=jax.ShapeDtypeStruct((n, D_MODEL), F32),
        compiler_params=_dense_params(),
        name="oproj_ffn1_final",
    )(x, attn, w_o, fn, w_gu, w_down, final_norm)


def _attn_kernel(slopes_ref, lam_ref, q_ref, k_ref, v_ref, sub_ref, o_ref):
    head = pl.program_id(1)
    qi = pl.program_id(2)
    slope = slopes_ref[head]

    q = q_ref[0]
    lane = lax.broadcasted_iota(jnp.int32, q.shape, 1)
    zero = jnp.zeros_like(q)
    qq = jnp.concatenate([jnp.where(lane < HEAD_DIM, q, zero),
                          jnp.where(lane >= HEAD_DIM, q, zero)], axis=0)

    row = lax.broadcasted_iota(jnp.int32, (Q_TILE, K_TILE), 0)
    col = lax.broadcasted_iota(jnp.int32, (Q_TILE, K_TILE), 1)
    rel = slope * (col - row).astype(F32)
    rel2 = jnp.concatenate([rel, rel], axis=0)
    visible = col <= row
    visible2 = jnp.concatenate([visible, visible], axis=0)

    def tile(kt, carry, diagonal):
        m, l, acc = carry
        ks = pl.multiple_of(kt * K_TILE, K_TILE)
        k = k_ref[0, pl.ds(ks, K_TILE), :]
        v = v_ref[0, pl.ds(ks, K_TILE), :]
        s = lax.dot_general(qq, k, (((1,), (1,)), ((), ())), preferred_element_type=F32)
        s = s + rel2
        if diagonal:
            s = jnp.where(visible2, s, -jnp.inf)
        off = slope * ((kt - qi) * K_TILE).astype(F32)
        m_new = jnp.maximum(m, jnp.max(s, axis=-1, keepdims=True) + off)
        alpha = jnp.exp(m - m_new)
        p = jnp.exp(s - (m_new - off))
        l = alpha * l + jnp.sum(p, axis=-1, keepdims=True)
        acc = alpha * acc + jnp.dot(p.astype(BF16), v, preferred_element_type=F32)
        return m_new, l, acc

    init = (jnp.full((2 * Q_TILE, 1), -jnp.inf, F32),
            jnp.zeros((2 * Q_TILE, 1), F32),
            jnp.zeros((2 * Q_TILE, V_DIM), F32))
    carry = lax.fori_loop(0, qi, lambda kt, c: tile(kt, c, False), init)
    _, l, acc = tile(qi, carry, True)

    lv = lam_ref[...]
    lam = (jnp.exp(jnp.sum(lv[0:1] * lv[1:2], axis=-1, keepdims=True))
           - jnp.exp(jnp.sum(lv[2:3] * lv[3:4], axis=-1, keepdims=True)) + LAMBDA_INIT)
    o = acc[:Q_TILE] / l[:Q_TILE] - lam * (acc[Q_TILE:] / l[Q_TILE:])
    o = o * _rms_scale(o) * sub_ref[...] * (1.0 - LAMBDA_INIT)
    o_ref[0] = o.astype(BF16)


def _diff_attention(slopes, lam_vecs, q, k, v, subln):
    b, s, _ = q.shape
    kv_spec = pl.BlockSpec((1, s, V_DIM), lambda bi, hi, qi: (bi, 0, hi))
    q_spec = pl.BlockSpec((1, Q_TILE, V_DIM), lambda bi, hi, qi: (bi, qi, hi))
    return pl.pallas_call(
        _attn_kernel,
        grid=(b, N_HEADS, s // Q_TILE),
        in_specs=[
            pl.BlockSpec(memory_space=pltpu.SMEM),
            pl.BlockSpec((4, HEAD_DIM), lambda bi, hi, qi: (0, 0)),
            q_spec, kv_spec, kv_spec,
            pl.BlockSpec((1, V_DIM), lambda bi, hi, qi: (0, 0)),
        ],
        out_specs=q_spec,
        out_shape=jax.ShapeDtypeStruct((b, s, N_HEADS * V_DIM), BF16),
        compiler_params=pltpu.CompilerParams(
            dimension_semantics=("arbitrary", "arbitrary", "arbitrary"),
            vmem_limit_bytes=V7X_VMEM_LIMIT_BYTES),
        name="diff_attention",
    )(slopes, lam_vecs, q, k, v, subln)


def kernel(x, a_norm, a_w_in, a_v_norm, a_w_sp, a_b_sp, a_w_out, ffn_norm, ffn_w_gu, ffn_w_down,
           kv_norm, kv_w, b_norm, b_w_q, b_lambda, b_subln, b_w_o, final_norm):
    b, s, d = x.shape
    assert d == D_MODEL and s % Q_TILE == 0 and (b * s) % ROW_TILE == 0
    xs = x.reshape(b * s, d)
    row = lambda g: g.reshape(1, -1).astype(F32)

    b_full = jnp.repeat(a_b_sp[0].T.astype(F32), GROUP_DIM, axis=1)
    xs = _gmlp_layer(xs, row(a_norm[0]), a_w_in[0].astype(BF16), row(a_v_norm[0]),
                     a_w_sp[0].astype(F32), b_full, a_w_out[0].astype(BF16))

    xs, q, k, v = _ffn_qkv_layer(xs, row(ffn_norm[0]), ffn_w_gu[0].astype(BF16),
                                 ffn_w_down[0].astype(BF16), row(kv_norm), kv_w.astype(BF16),
                                 row(b_norm[0]), b_w_q[0].astype(BF16))

    slopes = jnp.asarray(np.array([2.0 ** (-8.0 * (i + 1) / N_HEADS) for i in range(N_HEADS)],
                                  dtype=np.float32))
    attn = _diff_attention(slopes, b_lambda[0].astype(F32), q.reshape(b, s, d), k.reshape(b, s, d),
                           v.reshape(b, s, d), row(b_subln[0]))

    out = _oproj_ffn_final_layer(xs, attn.reshape(b * s, d), b_w_o[0].astype(BF16),
                                 row(ffn_norm[1]), ffn_w_gu[1].astype(BF16),
                                 ffn_w_down[1].astype(BF16), row(final_norm))
    return out.reshape(b, s, d)
```

```python
import math

import jax
import jax.numpy as jnp
import numpy as np
from jax import lax
from jax.experimental import pallas as pl
from jax.experimental.pallas import tpu as pltpu

D_MODEL = 1024
CHUNK = 128
N_GROUPS = 8
GROUP_DIM = D_MODEL // N_GROUPS
N_HEADS = 8
HEAD_DIM = 64
V_DIM = 2 * HEAD_DIM
D_FF = 2816
EPS = 1e-6
N_A = 1
LAMBDA_INIT = 0.8 - 0.6 * math.exp(-0.3 * N_A)
LOG2E = math.log2(math.e)

V7X_VMEM_LIMIT_BYTES = 56 * 1024 * 1024
BF16_EXACT_INT = 256
BF16_SUBLANES = 16

ROW_TILE = 256
Q_TILE = 512
K_TILE = 256
DIAG_TILES = Q_TILE // K_TILE
SUM_ROWS = BF16_SUBLANES

F32 = jnp.float32
BF16 = jnp.bfloat16


def _rms_scale(x):
    return lax.rsqrt(jnp.mean(x * x, axis=-1, keepdims=True) + EPS)


def _gelu_tanh(x):
    c = math.sqrt(2.0 / math.pi)
    return 0.5 * x * (1.0 + jnp.tanh(c * (x + 0.044715 * (x * x * x))))


def _silu(x):
    return 0.5 * x * (1.0 + jnp.tanh(0.5 * x))


def _const_spec(shape):
    nd = len(shape)
    return pl.BlockSpec(shape, lambda *_: (0,) * nd, pipeline_mode=pl.Buffered(1))


def _row_spec(width):
    return pl.BlockSpec((ROW_TILE, width), lambda i: (i, 0))


def _dense_params():
    return pltpu.CompilerParams(dimension_semantics=("arbitrary",),
                                vmem_limit_bytes=V7X_VMEM_LIMIT_BYTES)


def _gmlp_kernel(x_ref, an_ref, win_ref, vg_ref, wsp_ref, bsp_ref, wout_ref, o_ref):
    x = x_ref[...]
    h = (x * _rms_scale(x) * an_ref[...]).astype(BF16)
    uv = _gelu_tanh(jnp.dot(h, win_ref[...], preferred_element_type=F32))
    u = uv[:, :D_MODEL]
    v = uv[:, D_MODEL:]
    mu = jnp.mean(v, axis=-1, keepdims=True)
    vc = v - mu
    var = jnp.mean(vc * vc, axis=-1, keepdims=True)
    vn = (vc * lax.rsqrt(var + EPS) * vg_ref[...]).astype(BF16)

    t_idx = lax.broadcasted_iota(jnp.int32, (CHUNK, CHUNK), 0)
    s_idx = lax.broadcasted_iota(jnp.int32, (CHUNK, CHUNK), 1)
    causal = s_idx <= t_idx
    n_chunks = ROW_TILE // CHUNK
    cols = []
    for g in range(N_GROUPS):
        wm = jnp.where(causal, wsp_ref[g], 0.0).astype(BF16)
        bias = bsp_ref[:, g * GROUP_DIM:(g + 1) * GROUP_DIM]
        rows = []
        for c in range(n_chunks):
            vgc = vn[c * CHUNK:(c + 1) * CHUNK, g * GROUP_DIM:(g + 1) * GROUP_DIM]
            rows.append(jnp.dot(wm, vgc, preferred_element_type=F32) + bias)
        cols.append(jnp.concatenate(rows, axis=0))
    z = jnp.concatenate(cols, axis=1)
    gated = (u * z).astype(BF16)
    o_ref[...] = x + jnp.dot(gated, wout_ref[...], preferred_element_type=F32)


def _gmlp_layer(x, a_norm, w_in, v_gain, w_sp, b_full, w_out):
    n = x.shape[0]
    return pl.pallas_call(
        _gmlp_kernel,
        grid=(n // ROW_TILE,),
        in_specs=[
            _row_spec(D_MODEL),
            _const_spec((1, D_MODEL)),
            _const_spec((D_MODEL, 2 * D_MODEL)),
            _const_spec((1, D_MODEL)),
            _const_spec((N_GROUPS, CHUNK, CHUNK)),
            _const_spec((CHUNK, D_MODEL)),
            _const_spec((D_MODEL, D_MODEL)),
        ],
        out_specs=_row_spec(D_MODEL),
        out_shape=jax.ShapeDtypeStruct((n, D_MODEL), F32),
        compiler_params=_dense_params(),
        name="gmlp_mixer",
    )(x, a_norm, w_in, v_gain, w_sp, b_full, w_out)


def _swiglu(x, fn, wgu_ref, wd_ref):
    h = (x * _rms_scale(x) * fn).astype(BF16)
    gu = jnp.dot(h, wgu_ref[...], preferred_element_type=F32)
    a = (_silu(gu[:, :D_FF]) * gu[:, D_FF:]).astype(BF16)
    return x + jnp.dot(a, wd_ref[...], preferred_element_type=F32)


def _ffn_qkv_kernel(x_ref, fn_ref, wgu_ref, wd_ref, kvn_ref, wk_ref, wvt_ref, qn_ref, wqt_ref,
                    x_out, k_out, qt_out, vt_out):
    y = _swiglu(x_ref[...], fn_ref[...], wgu_ref, wd_ref)
    x_out[...] = y
    yn = y * _rms_scale(y)
    hk = (yn * kvn_ref[...]).astype(BF16)
    hq = (yn * qn_ref[...]).astype(BF16)
    k_out[...] = jnp.dot(hk, wk_ref[...], preferred_element_type=F32).astype(BF16)
    nt = (((1,), (1,)), ((), ()))
    vt_out[0] = lax.dot_general(wvt_ref[...], hk, nt, preferred_element_type=F32).astype(BF16)
    qt = lax.dot_general(wqt_ref[...], hq, nt, preferred_element_type=F32)
    qt_out[0] = (qt * (HEAD_DIM ** -0.5 * LOG2E)).astype(BF16)


def _ffn_qkv_layer(x, batch, fn, w_gu, w_down, kv_norm, w_k, w_vt, q_norm, w_qt):
    n = x.shape[0]
    seq = n // batch
    tiles_per_seq = seq // ROW_TILE
    t_spec = pl.BlockSpec((1, D_MODEL, ROW_TILE),
                          lambda i: (i // tiles_per_seq, 0, i % tiles_per_seq))
    t_shape = jax.ShapeDtypeStruct((batch, D_MODEL, seq), BF16)
    return pl.pallas_call(
        _ffn_qkv_kernel,
        grid=(n // ROW_TILE,),
        in_specs=[
            _row_spec(D_MODEL),
            _const_spec((1, D_MODEL)),
            _const_spec((D_MODEL, 2 * D_FF)),
            _const_spec((D_FF, D_MODEL)),
            _const_spec((1, D_MODEL)),
            _const_spec((D_MODEL, D_MODEL)),
            _const_spec((D_MODEL, D_MODEL)),
            _const_spec((1, D_MODEL)),
            _const_spec((D_MODEL, D_MODEL)),
        ],
        out_specs=[_row_spec(D_MODEL), _row_spec(D_MODEL), t_spec, t_spec],
        out_shape=[jax.ShapeDtypeStruct((n, D_MODEL), F32),
                   jax.ShapeDtypeStruct((n, D_MODEL), BF16), t_shape, t_shape],
        compiler_params=_dense_params(),
        name="ffn0_qkv",
    )(x, fn, w_gu, w_down, kv_norm, w_k, w_vt, q_norm, w_qt)


def _oproj_ffn_final_kernel(x_ref, a_ref, wo_ref, fn_ref, wgu_ref, wd_ref, gn_ref, o_ref):
    x = x_ref[...] + jnp.dot(a_ref[...], wo_ref[...], preferred_element_type=F32)
    y = _swiglu(x, fn_ref[...], wgu_ref, wd_ref)
    o_ref[...] = y * _rms_scale(y) * gn_ref[...]


def _oproj_ffn_final_layer(x, attn, w_o, fn, w_gu, w_down, final_norm):
    n = x.shape[0]
    return pl.pallas_call(
        _oproj_ffn_final_kernel,
        grid=(n // ROW_TILE,),
        in_specs=[
            _row_spec(D_MODEL),
            _row_spec(D_MODEL),
            _const_spec((D_MODEL, D_MODEL)),
            _const_spec((1, D_MODEL)),
            _const_spec((D_MODEL, 2 * D_FF)),
            _const_spec((D_FF, D_MODEL)),
            _const_spec((1, D_MODEL)),
        ],
        out_specs=_row_spec(D_MODEL),
        out_shape=jax.ShapeDtypeStruct((n, D_MODEL), F32),
        compiler_params=_dense_params(),
        name="oproj_ffn1_final",
    )(x, attn, w_o, fn, w_gu, w_down, final_norm)


def _attn_kernel(coef_ref, lam_ref, qt_ref, k_ref, vt_ref, sub_ref, o_ref,
                 kext_ref, qext_ref, mask_ref, s_ref, acc_ref):
    head = pl.program_id(1)
    qi = pl.program_id(2)
    c_parts = [coef_ref[head, i] for i in range(3)]
    c = coef_ref[head, 3]

    @pl.when(qi == 0)
    def _():
        lane = lax.broadcasted_iota(jnp.int32, (K_TILE, 128), 1)
        key_pos = lax.broadcasted_iota(jnp.int32, (K_TILE, 128), 0).astype(F32)
        kext = jnp.where(lane < 3, key_pos, 0.0)
        row = lax.broadcasted_iota(jnp.int32, (128, 2 * Q_TILE), 0)
        qry = lax.broadcasted_iota(jnp.int32, (128, 2 * Q_TILE), 1) % Q_TILE
        q_lo = -(qry % BF16_EXACT_INT).astype(F32)
        q_hi = -(qry - qry % BF16_EXACT_INT).astype(F32)
        qext = jnp.where((row >= 3) & (row < 6), q_lo, jnp.where((row >= 6) & (row < 9), q_hi, 0.0))
        for i in range(3):
            kext = jnp.where((lane == 3 + i) | (lane == 6 + i), c_parts[i], kext)
            qext = jnp.where(row == i, c_parts[i], qext)
        kext_ref[...] = kext.astype(BF16)
        qext_ref[...] = qext.astype(BF16)
        key = lax.broadcasted_iota(jnp.int32, (K_TILE, Q_TILE), 0)
        qcol = lax.broadcasted_iota(jnp.int32, (K_TILE, Q_TILE), 1)
        for d in range(DIAG_TILES):
            hide = jnp.where(key + d * K_TILE <= qcol, 0.0, -jnp.inf)
            mask_ref[d] = jnp.concatenate([hide, hide], axis=1)

    qt = qt_ref[0]
    feat = lax.broadcasted_iota(jnp.int32, qt.shape, 0)
    zero = jnp.zeros_like(qt)
    rhs = jnp.concatenate(
        [jnp.concatenate([jnp.where(feat < HEAD_DIM, qt, zero),
                          jnp.where(feat >= HEAD_DIM, qt, zero)], axis=1),
         qext_ref[...]], axis=0)
    ones_rows = jnp.ones((SUM_ROWS, K_TILE), BF16)

    def scores(kt, buf, diag=None):
        ks = pl.multiple_of(kt * K_TILE, K_TILE)
        lhs = jnp.concatenate([k_ref[0, pl.ds(ks, K_TILE), :], kext_ref[...]], axis=1)
        s = jnp.dot(lhs, rhs, preferred_element_type=F32)
        if diag is not None:
            s = s + mask_ref[diag]
        s_ref[buf] = s
        return jnp.max(s, axis=0, keepdims=True)

    def update(kt, buf, smax, m):
        ks = pl.multiple_of(kt * K_TILE, K_TILE)
        lhs = jnp.concatenate([vt_ref[0, :, pl.ds(ks, K_TILE)], ones_rows], axis=0)
        off = c * (kt * K_TILE - qi * Q_TILE).astype(F32)
        m_new = jnp.maximum(m, smax + off)
        alpha = jnp.exp2(m - m_new)
        p = jnp.exp2(s_ref[buf] - (m_new - off))
        acc_ref[...] = alpha * acc_ref[...] + jnp.dot(lhs, p.astype(BF16),
                                                      preferred_element_type=F32)
        return m_new

    first = qi * DIAG_TILES
    acc_ref[...] = jnp.zeros_like(acc_ref)
    m = jnp.full((1, 2 * Q_TILE), -1e30, F32)
    smax_b = scores(first + 1, 1, diag=1)
    smax_a = scores(first, 0, diag=0)
    m = update(first + 1, 1, smax_b, m)

    def pair(j, carry):
        smax_a, m = carry
        t = first - 2 * j
        smax_b = scores(t - 1, 1)
        m = update(t, 0, smax_a, m)
        smax_a = scores(t - 2, 0)
        m = update(t - 1, 1, smax_b, m)
        return smax_a, m

    smax_a, m = lax.fori_loop(0, qi, pair, (smax_a, m))
    update(0, 0, smax_a, m)

    acc = acc_ref[...]
    num = acc[:V_DIM]
    den = acc[V_DIM:V_DIM + 1]
    lv = lam_ref[...]
    lam = (jnp.exp(jnp.sum(lv[0:1] * lv[1:2], axis=-1, keepdims=True))
           - jnp.exp(jnp.sum(lv[2:3] * lv[3:4], axis=-1, keepdims=True)) + LAMBDA_INIT)
    o = (num[:, :Q_TILE] / den[:, :Q_TILE] - lam * (num[:, Q_TILE:] / den[:, Q_TILE:]))
    o = o * lax.rsqrt(jnp.mean(o * o, axis=0, keepdims=True) + EPS) * sub_ref[...]
    o_ref[0] = (o * (1.0 - LAMBDA_INIT)).T.astype(BF16)


def _alibi_coefficients():
    pieces = []
    rest = np.float64(LOG2E)
    for _ in range(3):
        piece = np.float64(np.asarray(rest, np.float32).astype(jnp.bfloat16).astype(np.float32))
        pieces.append(piece)
        rest = rest - piece
    rows = []
    for h in range(N_HEADS):
        slope = 2.0 ** (-8.0 * (h + 1) / N_HEADS)
        rows.append([slope * p for p in pieces] + [slope * sum(pieces)])
    return jnp.asarray(np.array(rows, dtype=np.float32))


def _diff_attention(lam_vecs, qt, k, vt, subln):
    b, s, _ = k.shape
    return pl.pallas_call(
        _attn_kernel,
        grid=(b, N_HEADS, s // Q_TILE),
        in_specs=[
            pl.BlockSpec(memory_space=pltpu.SMEM),
            pl.BlockSpec((4, HEAD_DIM), lambda bi, hi, qi: (0, 0)),
            pl.BlockSpec((1, V_DIM, Q_TILE), lambda bi, hi, qi: (bi, hi, qi)),
            pl.BlockSpec((1, s, V_DIM), lambda bi, hi, qi: (bi, 0, hi)),
            pl.BlockSpec((1, V_DIM, s), lambda bi, hi, qi: (bi, hi, 0)),
            pl.BlockSpec((V_DIM, Q_TILE), lambda bi, hi, qi: (0, 0)),
        ],
        out_specs=pl.BlockSpec((1, Q_TILE, V_DIM), lambda bi, hi, qi: (bi, qi, hi)),
        out_shape=jax.ShapeDtypeStruct((b, s, N_HEADS * V_DIM), BF16),
        scratch_shapes=[
            pltpu.VMEM((K_TILE, 128), BF16),
            pltpu.VMEM((128, 2 * Q_TILE), BF16),
            pltpu.VMEM((DIAG_TILES, K_TILE, 2 * Q_TILE), F32),
            pltpu.VMEM((2, K_TILE, 2 * Q_TILE), F32),
            pltpu.VMEM((V_DIM + SUM_ROWS, 2 * Q_TILE), F32),
        ],
        compiler_params=pltpu.CompilerParams(
            dimension_semantics=("arbitrary", "arbitrary", "arbitrary"),
            vmem_limit_bytes=V7X_VMEM_LIMIT_BYTES),
        name="diff_attention",
    )(_alibi_coefficients(), lam_vecs, qt, k, vt, subln)


def kernel(x, a_norm, a_w_in, a_v_norm, a_w_sp, a_b_sp, a_w_out, ffn_norm, ffn_w_gu, ffn_w_down,
           kv_norm, kv_w, b_norm, b_w_q, b_lambda, b_subln, b_w_o, final_norm):
    b, s, d = x.shape
    assert d == D_MODEL and s % Q_TILE == 0 and s % ROW_TILE == 0
    assert DIAG_TILES == 2 and K_TILE <= BF16_EXACT_INT
    xs = x.reshape(b * s, d)
    row = lambda g: g.reshape(1, -1).astype(F32)

    b_full = jnp.repeat(a_b_sp[0].T.astype(F32), GROUP_DIM, axis=1)
    xs = _gmlp_layer(xs, row(a_norm[0]), a_w_in[0].astype(BF16), row(a_v_norm[0]),
                     a_w_sp[0].astype(F32), b_full, a_w_out[0].astype(BF16))

    w_k = kv_w[:, :D_MODEL].astype(BF16)
    w_vt = kv_w[:, D_MODEL:].T.astype(BF16)
    w_qt = b_w_q[0].T.astype(BF16)
    xs, k, qt, vt = _ffn_qkv_layer(xs, b, row(ffn_norm[0]), ffn_w_gu[0].astype(BF16),
                                   ffn_w_down[0].astype(BF16), row(kv_norm), w_k, w_vt,
                                   row(b_norm[0]), w_qt)

    sub_cols = jnp.broadcast_to(b_subln[0].astype(F32)[:, None], (V_DIM, Q_TILE))
    attn = _diff_attention(b_lambda[0].astype(F32), qt, k.reshape(b, s, d), vt, sub_cols)

    out = _oproj_ffn_final_layer(xs, attn.reshape(b * s, d), b_w_o[0].astype(BF16),
                                 row(ffn_norm[1]), ffn_w_gu[1].astype(BF16),
                                 ffn_w_down[1].astype(BF16), row(final_norm))
    return out.reshape(b, s, d)
```

```python
import math

import jax
import jax.numpy as jnp
import numpy as np
from jax import lax
from jax.experimental import pallas as pl
from jax.experimental.pallas import tpu as pltpu

D_MODEL = 1024
CHUNK = 128
N_GROUPS = 8
GROUP_DIM = D_MODEL // N_GROUPS
N_HEADS = 8
HEAD_DIM = 64
V_DIM = 2 * HEAD_DIM
D_FF = 2816
EPS = 1e-6
N_A = 1
LAMBDA_INIT = 0.8 - 0.6 * math.exp(-0.3 * N_A)
LOG2E = math.log2(math.e)

V7X_VMEM_LIMIT_BYTES = 56 * 1024 * 1024
BF16_EXACT_INT = 256
BF16_SUBLANES = 16

ROW_TILE = 512
Q_TILE = 512
K_TILE = 256
DIAG_TILES = Q_TILE // K_TILE
SUM_ROWS = BF16_SUBLANES
NORM_CHUNK = 1024
SKIP_MARGIN = 160.0

F32 = jnp.float32
BF16 = jnp.bfloat16


def _rms_scale(x):
    return lax.rsqrt(jnp.mean(x * x, axis=-1, keepdims=True) + EPS)


def _gelu_tanh(x):
    c = math.sqrt(2.0 / math.pi)
    return 0.5 * x * (1.0 + jnp.tanh(c * (x + 0.044715 * (x * x * x))))


def _silu(x):
    return 0.5 * x * (1.0 + jnp.tanh(0.5 * x))


def _const_spec(shape):
    nd = len(shape)
    return pl.BlockSpec(shape, lambda *_: (0,) * nd, pipeline_mode=pl.Buffered(1))


def _row_spec(width):
    return pl.BlockSpec((ROW_TILE, width), lambda i: (i, 0))


def _dense_params():
    return pltpu.CompilerParams(dimension_semantics=("arbitrary",),
                                vmem_limit_bytes=V7X_VMEM_LIMIT_BYTES)


def _gmlp_kernel(x_ref, an_ref, win_ref, vg_ref, wsp_ref, bsp_ref, wout_ref, o_ref):
    x = x_ref[...]
    h = (x * _rms_scale(x) * an_ref[...]).astype(BF16)
    uv = _gelu_tanh(jnp.dot(h, win_ref[...], preferred_element_type=F32))
    u = uv[:, :D_MODEL]
    v = uv[:, D_MODEL:]
    mu = jnp.mean(v, axis=-1, keepdims=True)
    vc = v - mu
    var = jnp.mean(vc * vc, axis=-1, keepdims=True)
    vn = (vc * lax.rsqrt(var + EPS) * vg_ref[...]).astype(BF16)

    t_idx = lax.broadcasted_iota(jnp.int32, (CHUNK, CHUNK), 0)
    s_idx = lax.broadcasted_iota(jnp.int32, (CHUNK, CHUNK), 1)
    causal = s_idx <= t_idx
    n_chunks = ROW_TILE // CHUNK
    cols = []
    for g in range(N_GROUPS):
        wm = jnp.where(causal, wsp_ref[g], 0.0).astype(BF16)
        bias = bsp_ref[:, g * GROUP_DIM:(g + 1) * GROUP_DIM]
        rows = []
        for c in range(n_chunks):
            vgc = vn[c * CHUNK:(c + 1) * CHUNK, g * GROUP_DIM:(g + 1) * GROUP_DIM]
            rows.append(jnp.dot(wm, vgc, preferred_element_type=F32) + bias)
        cols.append(jnp.concatenate(rows, axis=0))
    z = jnp.concatenate(cols, axis=1)
    gated = (u * z).astype(BF16)
    o_ref[...] = x + jnp.dot(gated, wout_ref[...], preferred_element_type=F32)


def _gmlp_layer(x, a_norm, w_in, v_gain, w_sp, b_full, w_out):
    n = x.shape[0]
    return pl.pallas_call(
        _gmlp_kernel,
        grid=(n // ROW_TILE,),
        in_specs=[
            _row_spec(D_MODEL),
            _const_spec((1, D_MODEL)),
            _const_spec((D_MODEL, 2 * D_MODEL)),
            _const_spec((1, D_MODEL)),
            _const_spec((N_GROUPS, CHUNK, CHUNK)),
            _const_spec((CHUNK, D_MODEL)),
            _const_spec((D_MODEL, D_MODEL)),
        ],
        out_specs=_row_spec(D_MODEL),
        out_shape=jax.ShapeDtypeStruct((n, D_MODEL), F32),
        compiler_params=_dense_params(),
        name="gmlp_mixer",
    )(x, a_norm, w_in, v_gain, w_sp, b_full, w_out)


def _swiglu(x, fn, wgu_ref, wd_ref):
    h = (x * _rms_scale(x) * fn).astype(BF16)
    gu = jnp.dot(h, wgu_ref[...], preferred_element_type=F32)
    a = (_silu(gu[:, :D_FF]) * gu[:, D_FF:]).astype(BF16)
    return x + jnp.dot(a, wd_ref[...], preferred_element_type=F32)


def _ffn_qkv_kernel(x_ref, fn_ref, wgu_ref, wd_ref, kvn_ref, wk_ref, wvt_ref, qn_ref, wqt_ref,
                    x_out, k_out, qt_out, vt_out):
    y = _swiglu(x_ref[...], fn_ref[...], wgu_ref, wd_ref)
    x_out[...] = y
    yn = y * _rms_scale(y)
    hk = (yn * kvn_ref[...]).astype(BF16)
    hq = (yn * qn_ref[...]).astype(BF16)
    k_out[...] = jnp.dot(hk, wk_ref[...], preferred_element_type=F32).astype(BF16)
    nt = (((1,), (1,)), ((), ()))
    vt_out[0] = lax.dot_general(wvt_ref[...], hk, nt, preferred_element_type=F32).astype(BF16)
    qt = lax.dot_general(wqt_ref[...], hq, nt, preferred_element_type=F32)
    qt_out[0] = (qt * (HEAD_DIM ** -0.5 * LOG2E)).astype(BF16)


def _ffn_qkv_layer(x, batch, fn, w_gu, w_down, kv_norm, w_k, w_vt, q_norm, w_qt):
    n = x.shape[0]
    seq = n // batch
    tiles_per_seq = seq // ROW_TILE
    t_spec = pl.BlockSpec((1, D_MODEL, ROW_TILE),
                          lambda i: (i // tiles_per_seq, 0, i % tiles_per_seq))
    t_shape = jax.ShapeDtypeStruct((batch, D_MODEL, seq), BF16)
    return pl.pallas_call(
        _ffn_qkv_kernel,
        grid=(n // ROW_TILE,),
        in_specs=[
            _row_spec(D_MODEL),
            _const_spec((1, D_MODEL)),
            _const_spec((D_MODEL, 2 * D_FF)),
            _const_spec((D_FF, D_MODEL)),
            _const_spec((1, D_MODEL)),
            _const_spec((D_MODEL, D_MODEL)),
            _const_spec((D_MODEL, D_MODEL)),
            _const_spec((1, D_MODEL)),
            _const_spec((D_MODEL, D_MODEL)),
        ],
        out_specs=[_row_spec(D_MODEL), _row_spec(D_MODEL), t_spec, t_spec],
        out_shape=[jax.ShapeDtypeStruct((n, D_MODEL), F32),
                   jax.ShapeDtypeStruct((n, D_MODEL), BF16), t_shape, t_shape],
        compiler_params=_dense_params(),
        name="ffn0_qkv",
    )(x, fn, w_gu, w_down, kv_norm, w_k, w_vt, q_norm, w_qt)


def _oproj_ffn_final_kernel(x_ref, a_ref, wo_ref, fn_ref, wgu_ref, wd_ref, gn_ref, o_ref):
    x = x_ref[...] + jnp.dot(a_ref[...], wo_ref[...], preferred_element_type=F32)
    y = _swiglu(x, fn_ref[...], wgu_ref, wd_ref)
    o_ref[...] = y * _rms_scale(y) * gn_ref[...]


def _oproj_ffn_final_layer(x, attn, w_o, fn, w_gu, w_down, final_norm):
    n = x.shape[0]
    return pl.pallas_call(
        _oproj_ffn_final_kernel,
        grid=(n // ROW_TILE,),
        in_specs=[
            _row_spec(D_MODEL),
            _row_spec(D_MODEL),
            _const_spec((D_MODEL, D_MODEL)),
            _const_spec((1, D_MODEL)),
            _const_spec((D_MODEL, 2 * D_FF)),
            _const_spec((D_FF, D_MODEL)),
            _const_spec((1, D_MODEL)),
        ],
        out_specs=_row_spec(D_MODEL),
        out_shape=jax.ShapeDtypeStruct((n, D_MODEL), F32),
        compiler_params=_dense_params(),
        name="oproj_ffn1_final",
    )(x, attn, w_o, fn, w_gu, w_down, final_norm)


def _attn_kernel(coef_ref, lam_ref, qt_ref, k_ref, vt_ref, sub_ref, o_ref,
                 kext_ref, qext_ref, mask_ref, kmax2_ref, s_ref, acc_ref):
    head = pl.program_id(1)
    qi = pl.program_id(2)
    c_parts = [coef_ref[head, i] for i in range(3)]
    c = coef_ref[head, 3]

    @pl.when(qi == 0)
    def _():
        lane = lax.broadcasted_iota(jnp.int32, (K_TILE, 128), 1)
        key_pos = lax.broadcasted_iota(jnp.int32, (K_TILE, 128), 0).astype(F32)
        kext = jnp.where(lane < 3, key_pos, 0.0)
        row = lax.broadcasted_iota(jnp.int32, (128, 2 * Q_TILE), 0)
        qry = lax.broadcasted_iota(jnp.int32, (128, 2 * Q_TILE), 1) % Q_TILE
        q_lo = -(qry % BF16_EXACT_INT).astype(F32)
        q_hi = -(qry - qry % BF16_EXACT_INT).astype(F32)
        qext = jnp.where((row >= 3) & (row < 6), q_lo, jnp.where((row >= 6) & (row < 9), q_hi, 0.0))
        for i in range(3):
            kext = jnp.where((lane == 3 + i) | (lane == 6 + i), c_parts[i], kext)
            qext = jnp.where(row == i, c_parts[i], qext)
        kext_ref[...] = kext.astype(BF16)
        qext_ref[...] = qext.astype(BF16)
        key = lax.broadcasted_iota(jnp.int32, (K_TILE, Q_TILE), 0)
        qcol = lax.broadcasted_iota(jnp.int32, (K_TILE, Q_TILE), 1)
        for d in range(DIAG_TILES):
            hide = jnp.where(key + d * K_TILE <= qcol, 0.0, -jnp.inf)
            mask_ref[d] = jnp.concatenate([hide, hide], axis=1)

        same_half = ((lax.broadcasted_iota(jnp.int32, (V_DIM, V_DIM), 0) < HEAD_DIM)
                     == (lax.broadcasted_iota(jnp.int32, (V_DIM, V_DIM), 1) < HEAD_DIM))
        selector = jnp.where(same_half, 1.0, 0.0).astype(BF16)

        def norm_chunk(i, best):
            rows = k_ref[0, pl.ds(pl.multiple_of(i * NORM_CHUNK, NORM_CHUNK), NORM_CHUNK), :]
            sq = jnp.dot(rows * rows, selector, preferred_element_type=F32)
            return jnp.maximum(best, jnp.max(sq, axis=0, keepdims=True))

        kmax2 = lax.fori_loop(0, k_ref.shape[1] // NORM_CHUNK, norm_chunk,
                              jnp.zeros((1, V_DIM), F32))
        kmax2_ref[...] = jnp.concatenate(
            [jnp.broadcast_to(kmax2[:, 0:1], (1, Q_TILE)),
             jnp.broadcast_to(kmax2[:, HEAD_DIM:HEAD_DIM + 1], (1, Q_TILE))], axis=1)

    qt = qt_ref[0]
    feat = lax.broadcasted_iota(jnp.int32, qt.shape, 0)
    zero = jnp.zeros_like(qt)
    rhs = jnp.concatenate(
        [jnp.concatenate([jnp.where(feat < HEAD_DIM, qt, zero),
                          jnp.where(feat >= HEAD_DIM, qt, zero)], axis=1),
         qext_ref[...]], axis=0)
    ones_rows = jnp.ones((SUM_ROWS, K_TILE), BF16)

    def scores(kt, buf, diag=None):
        ks = pl.multiple_of(kt * K_TILE, K_TILE)
        lhs = jnp.concatenate([k_ref[0, pl.ds(ks, K_TILE), :], kext_ref[...]], axis=1)
        s = jnp.dot(lhs, rhs, preferred_element_type=F32)
        if diag is not None:
            s = s + mask_ref[diag]
        s_ref[buf] = s
        return jnp.max(s, axis=0, keepdims=True)

    def update(kt, buf, smax, m):
        ks = pl.multiple_of(kt * K_TILE, K_TILE)
        lhs = jnp.concatenate([vt_ref[0, :, pl.ds(ks, K_TILE)], ones_rows], axis=0)
        off = c * (kt * K_TILE - qi * Q_TILE).astype(F32)
        m_new = jnp.maximum(m, smax + off)
        alpha = jnp.exp2(m - m_new)
        p = jnp.exp2(s_ref[buf] - (m_new - off))
        acc_ref[...] = alpha * acc_ref[...] + jnp.dot(lhs, p.astype(BF16),
                                                      preferred_element_type=F32)
        return m_new

    first = qi * DIAG_TILES
    acc_ref[...] = jnp.zeros_like(acc_ref)
    m = jnp.full((1, 2 * Q_TILE), -1e30, F32)
    smax_b = scores(first + 1, 1, diag=1)
    smax_a = scores(first, 0, diag=0)
    m = update(first + 1, 1, smax_b, m)

    def pair(j, carry):
        smax_a, m = carry
        t = first - 2 * j
        smax_b = scores(t - 1, 1)
        m = update(t, 0, smax_a, m)
        smax_a = scores(t - 2, 0)
        m = update(t - 1, 1, smax_b, m)
        return smax_a, m

    q32 = qt.astype(F32)
    q32 = q32 * q32
    qn2 = jnp.concatenate([jnp.sum(q32[:HEAD_DIM], axis=0, keepdims=True),
                           jnp.sum(q32[HEAD_DIM:], axis=0, keepdims=True)], axis=1)
    bound = 1.02 * jnp.sqrt(qn2 * kmax2_ref[...])
    m_lb = jnp.maximum(m, smax_a)
    gap = jnp.max(bound - m_lb, axis=1, keepdims=True)
    reach = (gap + SKIP_MARGIN) / c
    pairs = jnp.floor(reach * (1.0 / (2 * K_TILE))) + 1.0
    pairs = jnp.clip(pairs, 0.0, float(2 ** 20)).astype(jnp.int32)
    n_pairs = jnp.minimum(qi, pairs[0, 0])

    smax_a, m = lax.fori_loop(0, n_pairs, pair, (smax_a, m))
    update(first - 2 * n_pairs, 0, smax_a, m)

    acc = acc_ref[...]
    num = acc[:V_DIM]
    den = acc[V_DIM:V_DIM + 1]
    lv = lam_ref[...]
    lam = (jnp.exp(jnp.sum(lv[0:1] * lv[1:2], axis=-1, keepdims=True))
           - jnp.exp(jnp.sum(lv[2:3] * lv[3:4], axis=-1, keepdims=True)) + LAMBDA_INIT)
    o = (num[:, :Q_TILE] / den[:, :Q_TILE] - lam * (num[:, Q_TILE:] / den[:, Q_TILE:]))
    o = o * lax.rsqrt(jnp.mean(o * o, axis=0, keepdims=True) + EPS) * sub_ref[...]
    o_ref[0] = (o * (1.0 - LAMBDA_INIT)).T.astype(BF16)


def _alibi_coefficients():
    pieces = []
    rest = np.float64(LOG2E)
    for _ in range(3):
        piece = np.float64(np.asarray(rest, np.float32).astype(jnp.bfloat16).astype(np.float32))
        pieces.append(piece)
        rest = rest - piece
    rows = []
    for h in range(N_HEADS):
        slope = 2.0 ** (-8.0 * (h + 1) / N_HEADS)
        rows.append([slope * p for p in pieces] + [slope * sum(pieces)])
    return jnp.asarray(np.array(rows, dtype=np.float32))


def _diff_attention(lam_vecs, qt, k, vt, subln):
    b, s, _ = k.shape
    return pl.pallas_call(
        _attn_kernel,
        grid=(b, N_HEADS, s // Q_TILE),
        in_specs=[
            pl.BlockSpec(memory_space=pltpu.SMEM),
            pl.BlockSpec((4, HEAD_DIM), lambda bi, hi, qi: (0, 0)),
            pl.BlockSpec((1, V_DIM, Q_TILE), lambda bi, hi, qi: (bi, hi, qi)),
            pl.BlockSpec((1, s, V_DIM), lambda bi, hi, qi: (bi, 0, hi)),
            pl.BlockSpec((1, V_DIM, s), lambda bi, hi, qi: (bi, hi, 0)),
            pl.BlockSpec((V_DIM, Q_TILE), lambda bi, hi, qi: (0, 0)),
        ],
        out_specs=pl.BlockSpec((1, Q_TILE, V_DIM), lambda bi, hi, qi: (bi, qi, hi)),
        out_shape=jax.ShapeDtypeStruct((b, s, N_HEADS * V_DIM), BF16),
        scratch_shapes=[
            pltpu.VMEM((K_TILE, 128), BF16),
            pltpu.VMEM((128, 2 * Q_TILE), BF16),
            pltpu.VMEM((DIAG_TILES, K_TILE, 2 * Q_TILE), F32),
            pltpu.VMEM((1, 2 * Q_TILE), F32),
            pltpu.VMEM((2, K_TILE, 2 * Q_TILE), F32),
            pltpu.VMEM((V_DIM + SUM_ROWS, 2 * Q_TILE), F32),
        ],
        compiler_params=pltpu.CompilerParams(
            dimension_semantics=("arbitrary", "arbitrary", "arbitrary"),
            vmem_limit_bytes=V7X_VMEM_LIMIT_BYTES),
        name="diff_attention",
    )(_alibi_coefficients(), lam_vecs, qt, k, vt, subln)


def kernel(x, a_norm, a_w_in, a_v_norm, a_w_sp, a_b_sp, a_w_out, ffn_norm, ffn_w_gu, ffn_w_down,
           kv_norm, kv_w, b_norm, b_w_q, b_lambda, b_subln, b_w_o, final_norm):
    b, s, d = x.shape
    assert d == D_MODEL and s % Q_TILE == 0 and s % ROW_TILE == 0
    assert DIAG_TILES == 2 and K_TILE <= BF16_EXACT_INT
    xs = x.reshape(b * s, d)
    row = lambda g: g.reshape(1, -1).astype(F32)

    b_full = jnp.repeat(a_b_sp[0].T.astype(F32), GROUP_DIM, axis=1)
    xs = _gmlp_layer(xs, row(a_norm[0]), a_w_in[0].astype(BF16), row(a_v_norm[0]),
                     a_w_sp[0].astype(F32), b_full, a_w_out[0].astype(BF16))

    w_k = kv_w[:, :D_MODEL].astype(BF16)
    w_vt = kv_w[:, D_MODEL:].T.astype(BF16)
    w_qt = b_w_q[0].T.astype(BF16)
    xs, k, qt, vt = _ffn_qkv_layer(xs, b, row(ffn_norm[0]), ffn_w_gu[0].astype(BF16),
                                   ffn_w_down[0].astype(BF16), row(kv_norm), w_k, w_vt,
                                   row(b_norm[0]), w_qt)

    sub_cols = jnp.broadcast_to(b_subln[0].astype(F32)[:, None], (V_DIM, Q_TILE))
    attn = _diff_attention(b_lambda[0].astype(F32), qt, k.reshape(b, s, d), vt, sub_cols)

    out = _oproj_ffn_final_layer(xs, attn.reshape(b * s, d), b_w_o[0].astype(BF16),
                                 row(ffn_norm[1]), ffn_w_gu[1].astype(BF16),
                                 ffn_w_down[1].astype(BF16), row(final_norm))
    return out.reshape(b, s, d)
```

```python
import math

import jax
import jax.numpy as jnp
import numpy as np
from jax import lax
from jax.experimental import pallas as pl
from jax.experimental.pallas import tpu as pltpu

D_MODEL = 1024
CHUNK = 128
N_GROUPS = 8
GROUP_DIM = D_MODEL // N_GROUPS
N_HEADS = 8
HEAD_DIM = 64
V_DIM = 2 * HEAD_DIM
D_FF = 2816
EPS = 1e-6
N_A = 1
LAMBDA_INIT = 0.8 - 0.6 * math.exp(-0.3 * N_A)
LOG2E = math.log2(math.e)

V7X_VMEM_LIMIT_BYTES = 56 * 1024 * 1024
BF16_EXACT_INT = 256
BF16_SUBLANES = 16

ROW_TILE = 512
Q_TILE = 512
K_TILE = 256
DIAG_TILES = Q_TILE // K_TILE
SUM_ROWS = BF16_SUBLANES
NORM_CHUNK = 1024
BATCH_BLOCK = 2
SKIP_MARGIN = 160.0

F32 = jnp.float32
BF16 = jnp.bfloat16


def _rms_scale(x):
    return lax.rsqrt(jnp.mean(x * x, axis=-1, keepdims=True) + EPS)


def _gelu_tanh(x):
    c = math.sqrt(2.0 / math.pi)
    return 0.5 * x * (1.0 + jnp.tanh(c * (x + 0.044715 * (x * x * x))))


def _silu(x):
    return 0.5 * x * (1.0 + jnp.tanh(0.5 * x))


def _const_spec(shape):
    nd = len(shape)
    return pl.BlockSpec(shape, lambda *_: (0,) * nd, pipeline_mode=pl.Buffered(1))


def _row_spec(width):
    return pl.BlockSpec((ROW_TILE, width), lambda i: (i, 0))


def _dense_params():
    return pltpu.CompilerParams(dimension_semantics=("arbitrary",),
                                vmem_limit_bytes=V7X_VMEM_LIMIT_BYTES)


def _gmlp_kernel(x_ref, an_ref, win_ref, vg_ref, wsp_ref, bsp_ref, wout_ref, o_ref):
    x = x_ref[...]
    h = (x * _rms_scale(x) * an_ref[...]).astype(BF16)
    uv = _gelu_tanh(jnp.dot(h, win_ref[...], preferred_element_type=F32))
    u = uv[:, :D_MODEL]
    v = uv[:, D_MODEL:]
    mu = jnp.mean(v, axis=-1, keepdims=True)
    vc = v - mu
    var = jnp.mean(vc * vc, axis=-1, keepdims=True)
    vn = (vc * lax.rsqrt(var + EPS) * vg_ref[...]).astype(BF16)

    t_idx = lax.broadcasted_iota(jnp.int32, (CHUNK, CHUNK), 0)
    s_idx = lax.broadcasted_iota(jnp.int32, (CHUNK, CHUNK), 1)
    causal = s_idx <= t_idx
    n_chunks = ROW_TILE // CHUNK
    cols = []
    for g in range(N_GROUPS):
        wm = jnp.where(causal, wsp_ref[g], 0.0).astype(BF16)
        bias = bsp_ref[:, g * GROUP_DIM:(g + 1) * GROUP_DIM]
        rows = []
        for c in range(n_chunks):
            vgc = vn[c * CHUNK:(c + 1) * CHUNK, g * GROUP_DIM:(g + 1) * GROUP_DIM]
            rows.append(jnp.dot(wm, vgc, preferred_element_type=F32) + bias)
        cols.append(jnp.concatenate(rows, axis=0))
    z = jnp.concatenate(cols, axis=1)
    gated = (u * z).astype(BF16)
    o_ref[...] = x + jnp.dot(gated, wout_ref[...], preferred_element_type=F32)


def _gmlp_layer(x, a_norm, w_in, v_gain, w_sp, b_full, w_out):
    n = x.shape[0]
    return pl.pallas_call(
        _gmlp_kernel,
        grid=(n // ROW_TILE,),
        in_specs=[
            _row_spec(D_MODEL),
            _const_spec((1, D_MODEL)),
            _const_spec((D_MODEL, 2 * D_MODEL)),
            _const_spec((1, D_MODEL)),
            _const_spec((N_GROUPS, CHUNK, CHUNK)),
            _const_spec((CHUNK, D_MODEL)),
            _const_spec((D_MODEL, D_MODEL)),
        ],
        out_specs=_row_spec(D_MODEL),
        out_shape=jax.ShapeDtypeStruct((n, D_MODEL), F32),
        compiler_params=_dense_params(),
        name="gmlp_mixer",
    )(x, a_norm, w_in, v_gain, w_sp, b_full, w_out)


def _swiglu(x, fn, wgu_ref, wd_ref):
    h = (x * _rms_scale(x) * fn).astype(BF16)
    gu = jnp.dot(h, wgu_ref[...], preferred_element_type=F32)
    a = (_silu(gu[:, :D_FF]) * gu[:, D_FF:]).astype(BF16)
    return x + jnp.dot(a, wd_ref[...], preferred_element_type=F32)


def _ffn_qkv_kernel(x_ref, fn_ref, wgu_ref, wd_ref, kvn_ref, wk_ref, wvt_ref, qn_ref, wqt_ref,
                    x_out, k_out, qt_out, vt_out):
    y = _swiglu(x_ref[...], fn_ref[...], wgu_ref, wd_ref)
    x_out[...] = y
    yn = y * _rms_scale(y)
    hk = (yn * kvn_ref[...]).astype(BF16)
    hq = (yn * qn_ref[...]).astype(BF16)
    k_out[...] = jnp.dot(hk, wk_ref[...], preferred_element_type=F32).astype(BF16)
    nt = (((1,), (1,)), ((), ()))
    vt_out[0] = lax.dot_general(wvt_ref[...], hk, nt, preferred_element_type=F32).astype(BF16)
    qt = lax.dot_general(wqt_ref[...], hq, nt, preferred_element_type=F32)
    qt_out[0] = (qt * (HEAD_DIM ** -0.5 * LOG2E)).astype(BF16)


def _ffn_qkv_layer(x, batch, fn, w_gu, w_down, kv_norm, w_k, w_vt, q_norm, w_qt):
    n = x.shape[0]
    seq = n // batch
    tiles_per_seq = seq // ROW_TILE
    t_spec = pl.BlockSpec((1, D_MODEL, ROW_TILE),
                          lambda i: (i // tiles_per_seq, 0, i % tiles_per_seq))
    t_shape = jax.ShapeDtypeStruct((batch, D_MODEL, seq), BF16)
    return pl.pallas_call(
        _ffn_qkv_kernel,
        grid=(n // ROW_TILE,),
        in_specs=[
            _row_spec(D_MODEL),
            _const_spec((1, D_MODEL)),
            _const_spec((D_MODEL, 2 * D_FF)),
            _const_spec((D_FF, D_MODEL)),
            _const_spec((1, D_MODEL)),
            _const_spec((D_MODEL, D_MODEL)),
            _const_spec((D_MODEL, D_MODEL)),
            _const_spec((1, D_MODEL)),
            _const_spec((D_MODEL, D_MODEL)),
        ],
        out_specs=[_row_spec(D_MODEL), _row_spec(D_MODEL), t_spec, t_spec],
        out_shape=[jax.ShapeDtypeStruct((n, D_MODEL), F32),
                   jax.ShapeDtypeStruct((n, D_MODEL), BF16), t_shape, t_shape],
        compiler_params=_dense_params(),
        name="ffn0_qkv",
    )(x, fn, w_gu, w_down, kv_norm, w_k, w_vt, q_norm, w_qt)


def _oproj_ffn_final_kernel(x_ref, a_ref, wo_ref, fn_ref, wgu_ref, wd_ref, gn_ref, o_ref):
    x = x_ref[...] + jnp.dot(a_ref[...], wo_ref[...], preferred_element_type=F32)
    y = _swiglu(x, fn_ref[...], wgu_ref, wd_ref)
    o_ref[...] = y * _rms_scale(y) * gn_ref[...]


def _oproj_ffn_final_layer(x, attn, w_o, fn, w_gu, w_down, final_norm):
    n = x.shape[0]
    return pl.pallas_call(
        _oproj_ffn_final_kernel,
        grid=(n // ROW_TILE,),
        in_specs=[
            _row_spec(D_MODEL),
            _row_spec(D_MODEL),
            _const_spec((D_MODEL, D_MODEL)),
            _const_spec((1, D_MODEL)),
            _const_spec((D_MODEL, 2 * D_FF)),
            _const_spec((D_FF, D_MODEL)),
            _const_spec((1, D_MODEL)),
        ],
        out_specs=_row_spec(D_MODEL),
        out_shape=jax.ShapeDtypeStruct((n, D_MODEL), F32),
        compiler_params=_dense_params(),
        name="oproj_ffn1_final",
    )(x, attn, w_o, fn, w_gu, w_down, final_norm)


def _attn_kernel(coef_ref, lam_ref, qt_ref, k_ref, vt_ref, sub_ref, o_ref,
                 kext_ref, qext_ref, mask_ref, kmax2_ref, s_ref, acc_ref):
    head = pl.program_id(1)
    qi = pl.program_id(2)
    streams = range(BATCH_BLOCK)
    c_parts = [coef_ref[head, i] for i in range(3)]
    c = coef_ref[head, 3]

    @pl.when(qi == 0)
    def _():
        lane = lax.broadcasted_iota(jnp.int32, (K_TILE, 128), 1)
        key_pos = lax.broadcasted_iota(jnp.int32, (K_TILE, 128), 0).astype(F32)
        kext = jnp.where(lane < 3, key_pos, 0.0)
        row = lax.broadcasted_iota(jnp.int32, (128, 2 * Q_TILE), 0)
        qry = lax.broadcasted_iota(jnp.int32, (128, 2 * Q_TILE), 1) % Q_TILE
        q_lo = -(qry % BF16_EXACT_INT).astype(F32)
        q_hi = -(qry - qry % BF16_EXACT_INT).astype(F32)
        qext = jnp.where((row >= 3) & (row < 6), q_lo, jnp.where((row >= 6) & (row < 9), q_hi, 0.0))
        for i in range(3):
            kext = jnp.where((lane == 3 + i) | (lane == 6 + i), c_parts[i], kext)
            qext = jnp.where(row == i, c_parts[i], qext)
        kext_ref[...] = kext.astype(BF16)
        qext_ref[...] = qext.astype(BF16)
        key = lax.broadcasted_iota(jnp.int32, (K_TILE, Q_TILE), 0)
        qcol = lax.broadcasted_iota(jnp.int32, (K_TILE, Q_TILE), 1)
        for d in range(DIAG_TILES):
            hide = jnp.where(key + d * K_TILE <= qcol, 0.0, -jnp.inf)
            mask_ref[d] = jnp.concatenate([hide, hide], axis=1)

        same_half = ((lax.broadcasted_iota(jnp.int32, (V_DIM, V_DIM), 0) < HEAD_DIM)
                     == (lax.broadcasted_iota(jnp.int32, (V_DIM, V_DIM), 1) < HEAD_DIM))
        selector = jnp.where(same_half, 1.0, 0.0).astype(BF16)

        def norm_chunk(i, best):
            start = pl.multiple_of(i * NORM_CHUNK, NORM_CHUNK)
            out = []
            for st in streams:
                rows = k_ref[st, pl.ds(start, NORM_CHUNK), :]
                sq = jnp.dot(rows * rows, selector, preferred_element_type=F32)
                out.append(jnp.maximum(best[st], jnp.max(sq, axis=0, keepdims=True)))
            return tuple(out)

        kmax2 = lax.fori_loop(0, k_ref.shape[1] // NORM_CHUNK, norm_chunk,
                              tuple(jnp.zeros((1, V_DIM), F32) for _ in streams))
        for st in streams:
            kmax2_ref[st] = jnp.concatenate(
                [jnp.broadcast_to(kmax2[st][:, 0:1], (1, Q_TILE)),
                 jnp.broadcast_to(kmax2[st][:, HEAD_DIM:HEAD_DIM + 1], (1, Q_TILE))], axis=1)

    ones_rows = jnp.ones((SUM_ROWS, K_TILE), BF16)
    rhs = []
    bound_max = None
    for st in streams:
        qt = qt_ref[st]
        feat = lax.broadcasted_iota(jnp.int32, qt.shape, 0)
        zero = jnp.zeros_like(qt)
        rhs.append(jnp.concatenate(
            [jnp.concatenate([jnp.where(feat < HEAD_DIM, qt, zero),
                              jnp.where(feat >= HEAD_DIM, qt, zero)], axis=1),
             qext_ref[...]], axis=0))
        q32 = qt.astype(F32)
        q32 = q32 * q32
        qn2 = jnp.concatenate([jnp.sum(q32[:HEAD_DIM], axis=0, keepdims=True),
                               jnp.sum(q32[HEAD_DIM:], axis=0, keepdims=True)], axis=1)
        top = jnp.max(qn2 * kmax2_ref[st], axis=1, keepdims=True)
        bound_max = top if bound_max is None else jnp.maximum(bound_max, top)

    reach = (2.04 * jnp.sqrt(bound_max) + SKIP_MARGIN) / c
    pairs = jnp.floor(reach * (1.0 / (2 * K_TILE))) + 1.0
    pairs = jnp.clip(pairs, 0.0, float(2 ** 20)).astype(jnp.int32)
    n_pairs = jnp.minimum(qi, pairs[0, 0])

    def scores(st, kt, buf, diag=None):
        ks = pl.multiple_of(kt * K_TILE, K_TILE)
        lhs = jnp.concatenate([k_ref[st, pl.ds(ks, K_TILE), :], kext_ref[...]], axis=1)
        s = jnp.dot(lhs, rhs[st], preferred_element_type=F32)
        if diag is not None:
            s = s + mask_ref[diag]
        s_ref[st, buf] = s
        return jnp.max(s, axis=0, keepdims=True)

    def update(st, kt, buf, smax, m):
        ks = pl.multiple_of(kt * K_TILE, K_TILE)
        lhs = jnp.concatenate([vt_ref[st, :, pl.ds(ks, K_TILE)], ones_rows], axis=0)
        off = c * (kt * K_TILE - qi * Q_TILE).astype(F32)
        m_new = jnp.maximum(m, smax + off)
        alpha = jnp.exp2(m - m_new)
        p = jnp.exp2(s_ref[st, buf] - (m_new - off))
        acc_ref[st] = alpha * acc_ref[st] + jnp.dot(lhs, p.astype(BF16),
                                                    preferred_element_type=F32)
        return m_new

    first = qi * DIAG_TILES
    acc_ref[...] = jnp.zeros_like(acc_ref)
    m = [jnp.full((1, 2 * Q_TILE), -1e30, F32) for _ in streams]
    smax_b = [scores(st, first + 1, 1, diag=1) for st in streams]
    smax_a = [scores(st, first, 0, diag=0) for st in streams]
    m = [update(st, first + 1, 1, smax_b[st], m[st]) for st in streams]

    def pair(j, carry):
        smax_a, m = carry
        t = first - 2 * j
        smax_b = [scores(st, t - 1, 1) for st in streams]
        m = [update(st, t, 0, smax_a[st], m[st]) for st in streams]
        smax_a = [scores(st, t - 2, 0) for st in streams]
        m = [update(st, t - 1, 1, smax_b[st], m[st]) for st in streams]
        return smax_a, m

    n_quads = n_pairs // 2
    carry = lax.fori_loop(0, n_quads, lambda j, cr: pair(2 * j + 1, pair(2 * j, cr)), (smax_a, m))
    smax_a, m = lax.fori_loop(2 * n_quads, n_pairs, pair, carry)
    for st in streams:
        update(st, first - 2 * n_pairs, 0, smax_a[st], m[st])

    lv = lam_ref[...]
    lam = (jnp.exp(jnp.sum(lv[0:1] * lv[1:2], axis=-1, keepdims=True))
           - jnp.exp(jnp.sum(lv[2:3] * lv[3:4], axis=-1, keepdims=True)) + LAMBDA_INIT)
    for st in streams:
        acc = acc_ref[st]
        num = acc[:V_DIM]
        den = acc[V_DIM:V_DIM + 1]
        o = (num[:, :Q_TILE] / den[:, :Q_TILE] - lam * (num[:, Q_TILE:] / den[:, Q_TILE:]))
        o = o * lax.rsqrt(jnp.mean(o * o, axis=0, keepdims=True) + EPS) * sub_ref[...]
        o_ref[st] = (o * (1.0 - LAMBDA_INIT)).T.astype(BF16)


def _alibi_coefficients():
    pieces = []
    rest = np.float64(LOG2E)
    for _ in range(3):
        piece = np.float64(np.asarray(rest, np.float32).astype(jnp.bfloat16).astype(np.float32))
        pieces.append(piece)
        rest = rest - piece
    rows = []
    for h in range(N_HEADS):
        slope = 2.0 ** (-8.0 * (h + 1) / N_HEADS)
        rows.append([slope * p for p in pieces] + [slope * sum(pieces)])
    return jnp.asarray(np.array(rows, dtype=np.float32))


def _diff_attention(lam_vecs, qt, k, vt, subln):
    b, s, _ = k.shape
    return pl.pallas_call(
        _attn_kernel,
        grid=(b // BATCH_BLOCK, N_HEADS, s // Q_TILE),
        in_specs=[
            pl.BlockSpec(memory_space=pltpu.SMEM),
            pl.BlockSpec((4, HEAD_DIM), lambda bi, hi, qi: (0, 0)),
            pl.BlockSpec((BATCH_BLOCK, V_DIM, Q_TILE), lambda bi, hi, qi: (bi, hi, qi)),
            pl.BlockSpec((BATCH_BLOCK, s, V_DIM), lambda bi, hi, qi: (bi, 0, hi)),
            pl.BlockSpec((BATCH_BLOCK, V_DIM, s), lambda bi, hi, qi: (bi, hi, 0)),
            pl.BlockSpec((V_DIM, Q_TILE), lambda bi, hi, qi: (0, 0)),
        ],
        out_specs=pl.BlockSpec((BATCH_BLOCK, Q_TILE, V_DIM), lambda bi, hi, qi: (bi, qi, hi)),
        out_shape=jax.ShapeDtypeStruct((b, s, N_HEADS * V_DIM), BF16),
        scratch_shapes=[
            pltpu.VMEM((K_TILE, 128), BF16),
            pltpu.VMEM((128, 2 * Q_TILE), BF16),
            pltpu.VMEM((DIAG_TILES, K_TILE, 2 * Q_TILE), F32),
            pltpu.VMEM((BATCH_BLOCK, 1, 2 * Q_TILE), F32),
            pltpu.VMEM((BATCH_BLOCK, 2, K_TILE, 2 * Q_TILE), F32),
            pltpu.VMEM((BATCH_BLOCK, V_DIM + SUM_ROWS, 2 * Q_TILE), F32),
        ],
        compiler_params=pltpu.CompilerParams(
            dimension_semantics=("arbitrary", "arbitrary", "arbitrary"),
            vmem_limit_bytes=V7X_VMEM_LIMIT_BYTES),
        name="diff_attention",
    )(_alibi_coefficients(), lam_vecs, qt, k, vt, subln)


def kernel(x, a_norm, a_w_in, a_v_norm, a_w_sp, a_b_sp, a_w_out, ffn_norm, ffn_w_gu, ffn_w_down,
           kv_norm, kv_w, b_norm, b_w_q, b_lambda, b_subln, b_w_o, final_norm):
    b, s, d = x.shape
    assert d == D_MODEL and s % Q_TILE == 0 and s % ROW_TILE == 0
    assert DIAG_TILES == 2 and K_TILE <= BF16_EXACT_INT and b % BATCH_BLOCK == 0
    xs = x.reshape(b * s, d)
    row = lambda g: g.reshape(1, -1).astype(F32)

    b_full = jnp.repeat(a_b_sp[0].T.astype(F32), GROUP_DIM, axis=1)
    xs = _gmlp_layer(xs, row(a_norm[0]), a_w_in[0].astype(BF16), row(a_v_norm[0]),
                     a_w_sp[0].astype(F32), b_full, a_w_out[0].astype(BF16))

    w_k = kv_w[:, :D_MODEL].astype(BF16)
    w_vt = kv_w[:, D_MODEL:].T.astype(BF16)
    w_qt = b_w_q[0].T.astype(BF16)
    xs, k, qt, vt = _ffn_qkv_layer(xs, b, row(ffn_norm[0]), ffn_w_gu[0].astype(BF16),
                                   ffn_w_down[0].astype(BF16), row(kv_norm), w_k, w_vt,
                                   row(b_norm[0]), w_qt)

    sub_cols = jnp.broadcast_to(b_subln[0].astype(F32)[:, None], (V_DIM, Q_TILE))
    attn = _diff_attention(b_lambda[0].astype(F32), qt, k.reshape(b, s, d), vt, sub_cols)

    out = _oproj_ffn_final_layer(xs, attn.reshape(b * s, d), b_w_o[0].astype(BF16),
                                 row(ffn_norm[1]), ffn_w_gu[1].astype(BF16),
                                 ffn_w_down[1].astype(BF16), row(final_norm))
    return out.reshape(b, s, d)
```

```python
import math

import jax
import jax.numpy as jnp
import numpy as np
from jax import lax
from jax.experimental import pallas as pl
from jax.experimental.pallas import tpu as pltpu

D_MODEL = 1024
CHUNK = 128
N_GROUPS = 8
GROUP_DIM = D_MODEL // N_GROUPS
N_HEADS = 8
HEAD_DIM = 64
V_DIM = 2 * HEAD_DIM
D_FF = 2816
EPS = 1e-6
N_A = 1
LAMBDA_INIT = 0.8 - 0.6 * math.exp(-0.3 * N_A)
LOG2E = math.log2(math.e)

V7X_VMEM_LIMIT_BYTES = 56 * 1024 * 1024
BF16_EXACT_INT = 256
BF16_SUBLANES = 16

ROW_TILE = 512
Q_TILE = 512
K_TILE = 256
DIAG_TILES = Q_TILE // K_TILE
SUM_ROWS = BF16_SUBLANES
NORM_CHUNK = 1024
BATCH_BLOCK = 2
SKIP_MARGIN = 160.0
FIXED_REFERENCE_SPREAD = 100.0

F32 = jnp.float32
BF16 = jnp.bfloat16


def _rms_scale(x):
    return lax.rsqrt(jnp.mean(x * x, axis=-1, keepdims=True) + EPS)


def _gelu_tanh(x):
    c = math.sqrt(2.0 / math.pi)
    return 0.5 * x * (1.0 + jnp.tanh(c * (x + 0.044715 * (x * x * x))))


def _silu(x):
    return 0.5 * x * (1.0 + jnp.tanh(0.5 * x))


def _const_spec(shape):
    nd = len(shape)
    return pl.BlockSpec(shape, lambda *_: (0,) * nd, pipeline_mode=pl.Buffered(1))


def _row_spec(width):
    return pl.BlockSpec((ROW_TILE, width), lambda i: (i, 0))


def _dense_params():
    return pltpu.CompilerParams(dimension_semantics=("arbitrary",),
                                vmem_limit_bytes=V7X_VMEM_LIMIT_BYTES)


def _gmlp_kernel(x_ref, an_ref, win_ref, vg_ref, wsp_ref, bsp_ref, wout_ref, o_ref):
    t_idx = lax.broadcasted_iota(jnp.int32, (CHUNK, CHUNK), 0)
    s_idx = lax.broadcasted_iota(jnp.int32, (CHUNK, CHUNK), 1)
    causal = s_idx <= t_idx
    n_chunks = ROW_TILE // CHUNK

    x = x_ref[...]
    h = (x * _rms_scale(x) * an_ref[...]).astype(BF16)
    uv = _gelu_tanh(jnp.dot(h, win_ref[...], preferred_element_type=F32))
    u = uv[:, :D_MODEL]
    v = uv[:, D_MODEL:]
    mu = jnp.mean(v, axis=-1, keepdims=True)
    vc = v - mu
    var = jnp.mean(vc * vc, axis=-1, keepdims=True)
    vn = (vc * lax.rsqrt(var + EPS) * vg_ref[...]).astype(BF16)
    cols = []
    for g in range(N_GROUPS):
        wm = jnp.where(causal, wsp_ref[g], 0.0).astype(BF16)
        bias = bsp_ref[:, g * GROUP_DIM:(g + 1) * GROUP_DIM]
        vg = jnp.concatenate([vn[c * CHUNK:(c + 1) * CHUNK, g * GROUP_DIM:(g + 1) * GROUP_DIM]
                              for c in range(n_chunks)], axis=1)
        zg = jnp.dot(wm, vg, preferred_element_type=F32)
        cols.append(jnp.concatenate([zg[:, c * GROUP_DIM:(c + 1) * GROUP_DIM] + bias
                                     for c in range(n_chunks)], axis=0))
    z = jnp.concatenate(cols, axis=1)
    gated = (u * z).astype(BF16)
    o_ref[...] = x + jnp.dot(gated, wout_ref[...], preferred_element_type=F32)


def _gmlp_layer(x, a_norm, w_in, v_gain, w_sp, b_full, w_out):
    n = x.shape[0]
    return pl.pallas_call(
        _gmlp_kernel,
        grid=(n // ROW_TILE,),
        in_specs=[
            _row_spec(D_MODEL),
            _const_spec((1, D_MODEL)),
            _const_spec((D_MODEL, 2 * D_MODEL)),
            _const_spec((1, D_MODEL)),
            _const_spec((N_GROUPS, CHUNK, CHUNK)),
            _const_spec((CHUNK, D_MODEL)),
            _const_spec((D_MODEL, D_MODEL)),
        ],
        out_specs=_row_spec(D_MODEL),
        out_shape=jax.ShapeDtypeStruct((n, D_MODEL), F32),
        compiler_params=_dense_params(),
        name="gmlp_mixer",
    )(x, a_norm, w_in, v_gain, w_sp, b_full, w_out)


def _swiglu(x, fn, wgu_ref, wd_ref):
    h = (x * _rms_scale(x) * fn).astype(BF16)
    gu = jnp.dot(h, wgu_ref[...], preferred_element_type=F32)
    a = (_silu(gu[:, :D_FF]) * gu[:, D_FF:]).astype(BF16)
    return x + jnp.dot(a, wd_ref[...], preferred_element_type=F32)


def _ffn_qkv_kernel(x_ref, fn_ref, wgu_ref, wd_ref, kvn_ref, wk_ref, wvt_ref, qn_ref, wqt_ref,
                    x_out, k_out, qt_out, vt_out):
    y = _swiglu(x_ref[...], fn_ref[...], wgu_ref, wd_ref)
    x_out[...] = y
    yn = y * _rms_scale(y)
    hk = (yn * kvn_ref[...]).astype(BF16)
    hq = (yn * qn_ref[...]).astype(BF16)
    k_out[...] = jnp.dot(hk, wk_ref[...], preferred_element_type=F32).astype(BF16)
    nt = (((1,), (1,)), ((), ()))
    vt_out[0] = lax.dot_general(wvt_ref[...], hk, nt, preferred_element_type=F32).astype(BF16)
    qt = lax.dot_general(wqt_ref[...], hq, nt, preferred_element_type=F32)
    qt_out[0] = (qt * (HEAD_DIM ** -0.5 * LOG2E)).astype(BF16)


def _ffn_qkv_layer(x, batch, fn, w_gu, w_down, kv_norm, w_k, w_vt, q_norm, w_qt):
    n = x.shape[0]
    seq = n // batch
    tiles_per_seq = seq // ROW_TILE
    t_spec = pl.BlockSpec((1, D_MODEL, ROW_TILE),
                          lambda i: (i // tiles_per_seq, 0, i % tiles_per_seq))
    t_shape = jax.ShapeDtypeStruct((batch, D_MODEL, seq), BF16)
    return pl.pallas_call(
        _ffn_qkv_kernel,
        grid=(n // ROW_TILE,),
        in_specs=[
            _row_spec(D_MODEL),
            _const_spec((1, D_MODEL)),
            _const_spec((D_MODEL, 2 * D_FF)),
            _const_spec((D_FF, D_MODEL)),
            _const_spec((1, D_MODEL)),
            _const_spec((D_MODEL, D_MODEL)),
            _const_spec((D_MODEL, D_MODEL)),
            _const_spec((1, D_MODEL)),
            _const_spec((D_MODEL, D_MODEL)),
        ],
        out_specs=[_row_spec(D_MODEL), _row_spec(D_MODEL), t_spec, t_spec],
        out_shape=[jax.ShapeDtypeStruct((n, D_MODEL), F32),
                   jax.ShapeDtypeStruct((n, D_MODEL), BF16), t_shape, t_shape],
        compiler_params=_dense_params(),
        name="ffn0_qkv",
    )(x, fn, w_gu, w_down, kv_norm, w_k, w_vt, q_norm, w_qt)


def _oproj_ffn_final_kernel(x_ref, a_ref, wo_ref, fn_ref, wgu_ref, wd_ref, gn_ref, o_ref):
    x = x_ref[...] + jnp.dot(a_ref[...], wo_ref[...], preferred_element_type=F32)
    y = _swiglu(x, fn_ref[...], wgu_ref, wd_ref)
    o_ref[...] = y * _rms_scale(y) * gn_ref[...]


def _oproj_ffn_final_layer(x, attn, w_o, fn, w_gu, w_down, final_norm):
    n = x.shape[0]
    return pl.pallas_call(
        _oproj_ffn_final_kernel,
        grid=(n // ROW_TILE,),
        in_specs=[
            _row_spec(D_MODEL),
            _row_spec(D_MODEL),
            _const_spec((D_MODEL, D_MODEL)),
            _const_spec((1, D_MODEL)),
            _const_spec((D_MODEL, 2 * D_FF)),
            _const_spec((D_FF, D_MODEL)),
            _const_spec((1, D_MODEL)),
        ],
        out_specs=_row_spec(D_MODEL),
        out_shape=jax.ShapeDtypeStruct((n, D_MODEL), F32),
        compiler_params=_dense_params(),
        name="oproj_ffn1_final",
    )(x, attn, w_o, fn, w_gu, w_down, final_norm)


def _attn_kernel(coef_ref, lam_ref, qt_ref, k_ref, vt_ref, sub_ref, o_ref,
                 kext_ref, qext_ref, mask_ref, kmax2_ref, s_ref, acc_ref):
    head = pl.program_id(1)
    qi = pl.program_id(2)
    streams = range(BATCH_BLOCK)
    c_parts = [coef_ref[head, i] for i in range(3)]
    c = coef_ref[head, 3]

    @pl.when(qi == 0)
    def _():
        lane = lax.broadcasted_iota(jnp.int32, (K_TILE, 128), 1)
        key_pos = lax.broadcasted_iota(jnp.int32, (K_TILE, 128), 0).astype(F32)
        kext = jnp.where(lane < 3, key_pos, 0.0)
        row = lax.broadcasted_iota(jnp.int32, (128, 2 * Q_TILE), 0)
        qry = lax.broadcasted_iota(jnp.int32, (128, 2 * Q_TILE), 1) % Q_TILE
        q_lo = -(qry % BF16_EXACT_INT).astype(F32)
        q_hi = -(qry - qry % BF16_EXACT_INT).astype(F32)
        qext = jnp.where((row >= 3) & (row < 6), q_lo, jnp.where((row >= 6) & (row < 9), q_hi, 0.0))
        for i in range(3):
            kext = jnp.where((lane == 3 + i) | (lane == 6 + i), c_parts[i], kext)
            qext = jnp.where(row == i, c_parts[i], qext)
        kext_ref[...] = kext.astype(BF16)
        qext_ref[...] = qext.astype(BF16)
        key = lax.broadcasted_iota(jnp.int32, (K_TILE, Q_TILE), 0)
        qcol = lax.broadcasted_iota(jnp.int32, (K_TILE, Q_TILE), 1)
        for d in range(DIAG_TILES):
            hide = jnp.where(key + d * K_TILE <= qcol, 0.0, -jnp.inf)
            mask_ref[d] = jnp.concatenate([hide, hide], axis=1)

        same_half = ((lax.broadcasted_iota(jnp.int32, (V_DIM, V_DIM), 0) < HEAD_DIM)
                     == (lax.broadcasted_iota(jnp.int32, (V_DIM, V_DIM), 1) < HEAD_DIM))
        selector = jnp.where(same_half, 1.0, 0.0).astype(BF16)

        def norm_chunk(i, best):
            start = pl.multiple_of(i * NORM_CHUNK, NORM_CHUNK)
            out = []
            for st in streams:
                rows = k_ref[st, pl.ds(start, NORM_CHUNK), :]
                sq = jnp.dot(rows * rows, selector, preferred_element_type=F32)
                out.append(jnp.maximum(best[st], jnp.max(sq, axis=0, keepdims=True)))
            return tuple(out)

        kmax2 = lax.fori_loop(0, k_ref.shape[1] // NORM_CHUNK, norm_chunk,
                              tuple(jnp.zeros((1, V_DIM), F32) for _ in streams))
        for st in streams:
            kmax2_ref[st] = jnp.concatenate(
                [jnp.broadcast_to(kmax2[st][:, 0:1], (1, Q_TILE)),
                 jnp.broadcast_to(kmax2[st][:, HEAD_DIM:HEAD_DIM + 1], (1, Q_TILE))], axis=1)

    ones_rows = jnp.ones((SUM_ROWS, K_TILE), BF16)
    rhs = []
    bound_max = None
    for st in streams:
        qt = qt_ref[st]
        feat = lax.broadcasted_iota(jnp.int32, qt.shape, 0)
        zero = jnp.zeros_like(qt)
        rhs.append(jnp.concatenate(
            [jnp.concatenate([jnp.where(feat < HEAD_DIM, qt, zero),
                              jnp.where(feat >= HEAD_DIM, qt, zero)], axis=1),
             qext_ref[...]], axis=0))
        q32 = qt.astype(F32)
        q32 = q32 * q32
        qn2 = jnp.concatenate([jnp.sum(q32[:HEAD_DIM], axis=0, keepdims=True),
                               jnp.sum(q32[HEAD_DIM:], axis=0, keepdims=True)], axis=1)
        top = jnp.max(qn2 * kmax2_ref[st], axis=1, keepdims=True)
        bound_max = top if bound_max is None else jnp.maximum(bound_max, top)

    spread = 2.04 * jnp.sqrt(bound_max)
    reach = (spread + SKIP_MARGIN) / c
    pairs = jnp.floor(reach * (1.0 / (2 * K_TILE))) + 1.0
    pairs = jnp.clip(pairs, 0.0, float(2 ** 20)).astype(jnp.int32)
    n_pairs = jnp.minimum(qi, pairs[0, 0])
    fixed_reference_ok = jnp.where(spread < FIXED_REFERENCE_SPREAD, 1, 0)[0, 0] == 1

    def scores(st, kt, buf, diag=None):
        ks = pl.multiple_of(kt * K_TILE, K_TILE)
        lhs = jnp.concatenate([k_ref[st, pl.ds(ks, K_TILE), :], kext_ref[...]], axis=1)
        s = jnp.dot(lhs, rhs[st], preferred_element_type=F32)
        if diag is not None:
            s = s + mask_ref[diag]
        s_ref[st, buf] = s
        return jnp.max(s, axis=0, keepdims=True)

    def update(st, kt, buf, smax, m):
        ks = pl.multiple_of(kt * K_TILE, K_TILE)
        lhs = jnp.concatenate([vt_ref[st, :, pl.ds(ks, K_TILE)], ones_rows], axis=0)
        off = c * (kt * K_TILE - qi * Q_TILE).astype(F32)
        m_new = jnp.maximum(m, smax + off)
        alpha = jnp.exp2(m - m_new)
        p = jnp.exp2(s_ref[st, buf] - (m_new - off))
        acc_ref[st] = alpha * acc_ref[st] + jnp.dot(lhs, p.astype(BF16),
                                                    preferred_element_type=F32)
        return m_new

    first = qi * DIAG_TILES
    acc_ref[...] = jnp.zeros_like(acc_ref)
    m = [jnp.full((1, 2 * Q_TILE), -1e30, F32) for _ in streams]
    smax_b = [scores(st, first + 1, 1, diag=1) for st in streams]
    smax_a = [scores(st, first, 0, diag=0) for st in streams]
    m = [update(st, first + 1, 1, smax_b[st], m[st]) for st in streams]

    def pair(j, carry):
        smax_a, m = carry
        t = first - 2 * j
        smax_b = [scores(st, t - 1, 1) for st in streams]
        m = [update(st, t, 0, smax_a[st], m[st]) for st in streams]
        smax_a = [scores(st, t - 2, 0) for st in streams]
        m = [update(st, t - 1, 1, smax_b[st], m[st]) for st in streams]
        return smax_a, m

    @pl.when(jnp.logical_not(fixed_reference_ok))
    def _():
        smax_last, m_last = lax.fori_loop(0, n_pairs, pair, (smax_a, m))
        for st in streams:
            update(st, first - 2 * n_pairs, 0, smax_last[st], m_last[st])

    @pl.when(fixed_reference_ok)
    def _():
        ref = [update(st, first, 0, smax_a[st], m[st]) for st in streams]

        def far_tiles(t0, count):
            for st in streams:
                probs, values = [], []
                for i in range(count):
                    kt = t0 - i
                    ks = pl.multiple_of(kt * K_TILE, K_TILE)
                    lhs = jnp.concatenate([k_ref[st, pl.ds(ks, K_TILE), :], kext_ref[...]], axis=1)
                    s = jnp.dot(lhs, rhs[st], preferred_element_type=F32)
                    off = c * (kt * K_TILE - qi * Q_TILE).astype(F32)
                    probs.append(jnp.exp2(s - (ref[st] - off)).astype(BF16))
                    values.append(jnp.concatenate([vt_ref[st, :, pl.ds(ks, K_TILE)], ones_rows],
                                                  axis=0))
                acc_ref[st] += jnp.dot(jnp.concatenate(values, axis=1),
                                       jnp.concatenate(probs, axis=0),
                                       preferred_element_type=F32)

        n_quads = n_pairs // 2

        def quad(j, carry):
            far_tiles(first - 1 - 4 * j, 4)
            return carry

        lax.fori_loop(0, n_quads, quad, 0)

        @pl.when(n_pairs % 2 == 1)
        def _():
            far_tiles(first - 1 - 4 * n_quads, 2)

    lv = lam_ref[...]
    lam = (jnp.exp(jnp.sum(lv[0:1] * lv[1:2], axis=-1, keepdims=True))
           - jnp.exp(jnp.sum(lv[2:3] * lv[3:4], axis=-1, keepdims=True)) + LAMBDA_INIT)
    for st in streams:
        acc = acc_ref[st]
        num = acc[:V_DIM]
        den = acc[V_DIM:V_DIM + 1]
        o = (num[:, :Q_TILE] / den[:, :Q_TILE] - lam * (num[:, Q_TILE:] / den[:, Q_TILE:]))
        o = o * lax.rsqrt(jnp.mean(o * o, axis=0, keepdims=True) + EPS) * sub_ref[...]
        o_ref[st] = (o * (1.0 - LAMBDA_INIT)).T.astype(BF16)


def _alibi_coefficients():
    pieces = []
    rest = np.float64(LOG2E)
    for _ in range(3):
        piece = np.float64(np.asarray(rest, np.float32).astype(jnp.bfloat16).astype(np.float32))
        pieces.append(piece)
        rest = rest - piece
    rows = []
    for h in range(N_HEADS):
        slope = 2.0 ** (-8.0 * (h + 1) / N_HEADS)
        rows.append([slope * p for p in pieces] + [slope * sum(pieces)])
    return jnp.asarray(np.array(rows, dtype=np.float32))


def _diff_attention(lam_vecs, qt, k, vt, subln):
    b, s, _ = k.shape
    return pl.pallas_call(
        _attn_kernel,
        grid=(b // BATCH_BLOCK, N_HEADS, s // Q_TILE),
        in_specs=[
            pl.BlockSpec(memory_space=pltpu.SMEM),
            pl.BlockSpec((4, HEAD_DIM), lambda bi, hi, qi: (0, 0)),
            pl.BlockSpec((BATCH_BLOCK, V_DIM, Q_TILE), lambda bi, hi, qi: (bi, hi, qi)),
            pl.BlockSpec((BATCH_BLOCK, s, V_DIM), lambda bi, hi, qi: (bi, 0, hi)),
            pl.BlockSpec((BATCH_BLOCK, V_DIM, s), lambda bi, hi, qi: (bi, hi, 0)),
            pl.BlockSpec((V_DIM, Q_TILE), lambda bi, hi, qi: (0, 0)),
        ],
        out_specs=pl.BlockSpec((BATCH_BLOCK, Q_TILE, V_DIM), lambda bi, hi, qi: (bi, qi, hi)),
        out_shape=jax.ShapeDtypeStruct((b, s, N_HEADS * V_DIM), BF16),
        scratch_shapes=[
            pltpu.VMEM((K_TILE, 128), BF16),
            pltpu.VMEM((128, 2 * Q_TILE), BF16),
            pltpu.VMEM((DIAG_TILES, K_TILE, 2 * Q_TILE), F32),
            pltpu.VMEM((BATCH_BLOCK, 1, 2 * Q_TILE), F32),
            pltpu.VMEM((BATCH_BLOCK, 2, K_TILE, 2 * Q_TILE), F32),
            pltpu.VMEM((BATCH_BLOCK, V_DIM + SUM_ROWS, 2 * Q_TILE), F32),
        ],
        compiler_params=pltpu.CompilerParams(
            dimension_semantics=("arbitrary", "arbitrary", "arbitrary"),
            vmem_limit_bytes=V7X_VMEM_LIMIT_BYTES),
        name="diff_attention",
    )(_alibi_coefficients(), lam_vecs, qt, k, vt, subln)


def kernel(x, a_norm, a_w_in, a_v_norm, a_w_sp, a_b_sp, a_w_out, ffn_norm, ffn_w_gu, ffn_w_down,
           kv_norm, kv_w, b_norm, b_w_q, b_lambda, b_subln, b_w_o, final_norm):
    b, s, d = x.shape
    assert d == D_MODEL and s % Q_TILE == 0 and s % ROW_TILE == 0
    assert DIAG_TILES == 2 and K_TILE <= BF16_EXACT_INT and b % BATCH_BLOCK == 0
    xs = x.reshape(b * s, d)
    row = lambda g: g.reshape(1, -1).astype(F32)

    b_full = jnp.repeat(a_b_sp[0].T.astype(F32), GROUP_DIM, axis=1)
    xs = _gmlp_layer(xs, row(a_norm[0]), a_w_in[0].astype(BF16), row(a_v_norm[0]),
                     a_w_sp[0].astype(F32), b_full, a_w_out[0].astype(BF16))

    w_k = kv_w[:, :D_MODEL].astype(BF16)
    w_vt = kv_w[:, D_MODEL:].T.astype(BF16)
    w_qt = b_w_q[0].T.astype(BF16)
    xs, k, qt, vt = _ffn_qkv_layer(xs, b, row(ffn_norm[0]), ffn_w_gu[0].astype(BF16),
                                   ffn_w_down[0].astype(BF16), row(kv_norm), w_k, w_vt,
                                   row(b_norm[0]), w_qt)

    sub_cols = jnp.broadcast_to(b_subln[0].astype(F32)[:, None], (V_DIM, Q_TILE))
    attn = _diff_attention(b_lambda[0].astype(F32), qt, k.reshape(b, s, d), vt, sub_cols)

    out = _oproj_ffn_final_layer(xs, attn.reshape(b * s, d), b_w_o[0].astype(BF16),
                                 row(ffn_norm[1]), ffn_w_gu[1].astype(BF16),
                                 ffn_w_down[1].astype(BF16), row(final_norm))
    return out.reshape(b, s, d)
```

```python
import math

import jax
import jax.numpy as jnp
import numpy as np
from jax import lax
from jax.experimental import pallas as pl
from jax.experimental.pallas import tpu as pltpu

D_MODEL = 1024
CHUNK = 128
N_GROUPS = 8
GROUP_DIM = D_MODEL // N_GROUPS
N_HEADS = 8
HEAD_DIM = 64
V_DIM = 2 * HEAD_DIM
D_FF = 2816
EPS = 1e-6
N_A = 1
LAMBDA_INIT = 0.8 - 0.6 * math.exp(-0.3 * N_A)
LOG2E = math.log2(math.e)

V7X_VMEM_LIMIT_BYTES = 56 * 1024 * 1024
BF16_EXACT_INT = 256
BF16_SUBLANES = 16

ROW_TILE = 512
Q_TILE = 512
K_TILE = 256
DIAG_TILES = Q_TILE // K_TILE
SUM_ROWS = BF16_SUBLANES
NORM_CHUNK = 1024
BATCH_BLOCK = 2
SKIP_MARGIN = 160.0
FIXED_REFERENCE_SPREAD = 100.0

F32 = jnp.float32
BF16 = jnp.bfloat16


def _rms_scale(x):
    return lax.rsqrt(jnp.mean(x * x, axis=-1, keepdims=True) + EPS)


def _gelu_tanh(x):
    c = math.sqrt(2.0 / math.pi)
    return 0.5 * x * (1.0 + jnp.tanh(c * (x + 0.044715 * (x * x * x))))


def _silu(x):
    return 0.5 * x * (1.0 + jnp.tanh(0.5 * x))


def _const_spec(shape):
    nd = len(shape)
    return pl.BlockSpec(shape, lambda *_: (0,) * nd, pipeline_mode=pl.Buffered(1))


def _row_spec(width):
    return pl.BlockSpec((ROW_TILE, width), lambda i: (i, 0))


def _dense_params():
    return pltpu.CompilerParams(dimension_semantics=("arbitrary",),
                                vmem_limit_bytes=V7X_VMEM_LIMIT_BYTES)


def _gmlp_kernel(x_ref, an_ref, win_ref, vg_ref, wsp_ref, bsp_ref, wout_ref, o_ref):
    t_idx = lax.broadcasted_iota(jnp.int32, (CHUNK, CHUNK), 0)
    s_idx = lax.broadcasted_iota(jnp.int32, (CHUNK, CHUNK), 1)
    causal = s_idx <= t_idx
    n_chunks = ROW_TILE // CHUNK

    x = x_ref[...]
    h = (x * _rms_scale(x) * an_ref[...]).astype(BF16)
    uv = _gelu_tanh(jnp.dot(h, win_ref[...], preferred_element_type=F32))
    u = uv[:, :D_MODEL]
    v = uv[:, D_MODEL:]
    mu = jnp.mean(v, axis=-1, keepdims=True)
    vc = v - mu
    var = jnp.mean(vc * vc, axis=-1, keepdims=True)
    vn = (vc * lax.rsqrt(var + EPS) * vg_ref[...]).astype(BF16)
    cols = []
    for g in range(N_GROUPS):
        wm = jnp.where(causal, wsp_ref[g], 0.0).astype(BF16)
        bias = bsp_ref[:, g * GROUP_DIM:(g + 1) * GROUP_DIM]
        vg = jnp.concatenate([vn[c * CHUNK:(c + 1) * CHUNK, g * GROUP_DIM:(g + 1) * GROUP_DIM]
                              for c in range(n_chunks)], axis=1)
        zg = jnp.dot(wm, vg, preferred_element_type=F32)
        cols.append(jnp.concatenate([zg[:, c * GROUP_DIM:(c + 1) * GROUP_DIM] + bias
                                     for c in range(n_chunks)], axis=0))
    z = jnp.concatenate(cols, axis=1)
    gated = (u * z).astype(BF16)
    o_ref[...] = x + jnp.dot(gated, wout_ref[...], preferred_element_type=F32)


def _gmlp_layer(x, a_norm, w_in, v_gain, w_sp, b_full, w_out):
    n = x.shape[0]
    return pl.pallas_call(
        _gmlp_kernel,
        grid=(n // ROW_TILE,),
        in_specs=[
            _row_spec(D_MODEL),
            _const_spec((1, D_MODEL)),
            _const_spec((D_MODEL, 2 * D_MODEL)),
            _const_spec((1, D_MODEL)),
            _const_spec((N_GROUPS, CHUNK, CHUNK)),
            _const_spec((CHUNK, D_MODEL)),
            _const_spec((D_MODEL, D_MODEL)),
        ],
        out_specs=_row_spec(D_MODEL),
        out_shape=jax.ShapeDtypeStruct((n, D_MODEL), F32),
        compiler_params=_dense_params(),
        name="gmlp_mixer",
    )(x, a_norm, w_in, v_gain, w_sp, b_full, w_out)


def _swiglu(x, fn, wgu_ref, wd_ref):
    h = (x * _rms_scale(x) * fn).astype(BF16)
    gu = jnp.dot(h, wgu_ref[...], preferred_element_type=F32)
    a = (_silu(gu[:, :D_FF]) * gu[:, D_FF:]).astype(BF16)
    return x + jnp.dot(a, wd_ref[...], preferred_element_type=F32)


def _ffn_qkv_kernel(x_ref, fn_ref, wgu_ref, wd_ref, kvn_ref, wk_ref, wvt_ref, qn_ref, wqt_ref,
                    x_out, k_out, qt_out, vt_out):
    y = _swiglu(x_ref[...], fn_ref[...], wgu_ref, wd_ref)
    x_out[...] = y
    yn = y * _rms_scale(y)
    hk = (yn * kvn_ref[...]).astype(BF16)
    hq = (yn * qn_ref[...]).astype(BF16)
    k_out[...] = jnp.dot(hk, wk_ref[...], preferred_element_type=F32).astype(BF16)
    nt = (((1,), (1,)), ((), ()))
    vt_out[0] = lax.dot_general(wvt_ref[...], hk, nt, preferred_element_type=F32).astype(BF16)
    qt = lax.dot_general(wqt_ref[...], hq, nt, preferred_element_type=F32)
    qt_out[0] = (qt * (HEAD_DIM ** -0.5 * LOG2E)).astype(BF16)


def _ffn_qkv_layer(x, batch, fn, w_gu, w_down, kv_norm, w_k, w_vt, q_norm, w_qt):
    n = x.shape[0]
    seq = n // batch
    tiles_per_seq = seq // ROW_TILE
    t_spec = pl.BlockSpec((1, D_MODEL, ROW_TILE),
                          lambda i: (i // tiles_per_seq, 0, i % tiles_per_seq))
    t_shape = jax.ShapeDtypeStruct((batch, D_MODEL, seq), BF16)
    return pl.pallas_call(
        _ffn_qkv_kernel,
        grid=(n // ROW_TILE,),
        in_specs=[
            _row_spec(D_MODEL),
            _const_spec((1, D_MODEL)),
            _const_spec((D_MODEL, 2 * D_FF)),
            _const_spec((D_FF, D_MODEL)),
            _const_spec((1, D_MODEL)),
            _const_spec((D_MODEL, D_MODEL)),
            _const_spec((D_MODEL, D_MODEL)),
            _const_spec((1, D_MODEL)),
            _const_spec((D_MODEL, D_MODEL)),
        ],
        out_specs=[_row_spec(D_MODEL), _row_spec(D_MODEL), t_spec, t_spec],
        out_shape=[jax.ShapeDtypeStruct((n, D_MODEL), F32),
                   jax.ShapeDtypeStruct((n, D_MODEL), BF16), t_shape, t_shape],
        compiler_params=_dense_params(),
        name="ffn0_qkv",
    )(x, fn, w_gu, w_down, kv_norm, w_k, w_vt, q_norm, w_qt)


def _oproj_ffn_final_kernel(x_ref, a_ref, wo_ref, fn_ref, wgu_ref, wd_ref, gn_ref, o_ref):
    x = x_ref[...] + lax.dot_general(a_ref[0], wo_ref[...], (((0,), (0,)), ((), ())),
                                     preferred_element_type=F32)
    y = _swiglu(x, fn_ref[...], wgu_ref, wd_ref)
    o_ref[...] = y * _rms_scale(y) * gn_ref[...]


def _oproj_ffn_final_layer(x, attn_t, w_o, fn, w_gu, w_down, final_norm):
    n = x.shape[0]
    tiles_per_seq = attn_t.shape[2] // ROW_TILE
    return pl.pallas_call(
        _oproj_ffn_final_kernel,
        grid=(n // ROW_TILE,),
        in_specs=[
            _row_spec(D_MODEL),
            pl.BlockSpec((1, D_MODEL, ROW_TILE),
                         lambda i: (i // tiles_per_seq, 0, i % tiles_per_seq)),
            _const_spec((D_MODEL, D_MODEL)),
            _const_spec((1, D_MODEL)),
            _const_spec((D_MODEL, 2 * D_FF)),
            _const_spec((D_FF, D_MODEL)),
            _const_spec((1, D_MODEL)),
        ],
        out_specs=_row_spec(D_MODEL),
        out_shape=jax.ShapeDtypeStruct((n, D_MODEL), F32),
        compiler_params=_dense_params(),
        name="oproj_ffn1_final",
    )(x, attn_t, w_o, fn, w_gu, w_down, final_norm)


def _attn_kernel(coef_ref, lam_ref, qt_ref, qall_ref, k_ref, vt_ref, sub_ref, o_ref,
                 kext_ref, qext_ref, mask_ref, kmax2_ref, s_ref, acc_ref, ctrl_ref):
    head = pl.program_id(1)
    qi = pl.program_id(2)
    streams = range(BATCH_BLOCK)
    c_parts = [coef_ref[head, i] for i in range(3)]
    c = coef_ref[head, 3]

    @pl.when(qi == 0)
    def _():
        lane = lax.broadcasted_iota(jnp.int32, (K_TILE, 128), 1)
        key_pos = lax.broadcasted_iota(jnp.int32, (K_TILE, 128), 0).astype(F32)
        kext = jnp.where(lane < 3, key_pos, 0.0)
        row = lax.broadcasted_iota(jnp.int32, (128, 2 * Q_TILE), 0)
        qry = lax.broadcasted_iota(jnp.int32, (128, 2 * Q_TILE), 1) % Q_TILE
        q_lo = -(qry % BF16_EXACT_INT).astype(F32)
        q_hi = -(qry - qry % BF16_EXACT_INT).astype(F32)
        qext = jnp.where((row >= 3) & (row < 6), q_lo, jnp.where((row >= 6) & (row < 9), q_hi, 0.0))
        for i in range(3):
            kext = jnp.where((lane == 3 + i) | (lane == 6 + i), c_parts[i], kext)
            qext = jnp.where(row == i, c_parts[i], qext)
        kext_ref[...] = kext.astype(BF16)
        qext_ref[...] = qext.astype(BF16)
        key = lax.broadcasted_iota(jnp.int32, (K_TILE, Q_TILE), 0)
        qcol = lax.broadcasted_iota(jnp.int32, (K_TILE, Q_TILE), 1)
        for d in range(DIAG_TILES):
            hide = jnp.where(key + d * K_TILE <= qcol, 0.0, -jnp.inf)
            mask_ref[d] = jnp.concatenate([hide, hide], axis=1)

        same_half = ((lax.broadcasted_iota(jnp.int32, (V_DIM, V_DIM), 0) < HEAD_DIM)
                     == (lax.broadcasted_iota(jnp.int32, (V_DIM, V_DIM), 1) < HEAD_DIM))
        selector = jnp.where(same_half, 1.0, 0.0).astype(BF16)

        def norm_chunk(i, best):
            start = pl.multiple_of(i * NORM_CHUNK, NORM_CHUNK)
            out = []
            for st in streams:
                rows = k_ref[st, pl.ds(start, NORM_CHUNK), :]
                sq = jnp.dot(rows * rows, selector, preferred_element_type=F32)
                out.append(jnp.maximum(best[st], jnp.max(sq, axis=0, keepdims=True)))
            return tuple(out)

        kmax2 = lax.fori_loop(0, k_ref.shape[1] // NORM_CHUNK, norm_chunk,
                              tuple(jnp.zeros((1, V_DIM), F32) for _ in streams))
        for st in streams:
            kmax2_ref[st] = jnp.concatenate(
                [jnp.broadcast_to(kmax2[st][:, 0:1], (1, Q_TILE)),
                 jnp.broadcast_to(kmax2[st][:, HEAD_DIM:HEAD_DIM + 1], (1, Q_TILE))], axis=1)

        def query_chunk(i, best):
            start = pl.multiple_of(i * NORM_CHUNK, NORM_CHUNK)
            for st in streams:
                q32 = qall_ref[st, :, pl.ds(start, NORM_CHUNK)].astype(F32)
                q32 = q32 * q32
                prod = jnp.concatenate(
                    [jnp.sum(q32[:HEAD_DIM], axis=0, keepdims=True) * kmax2[st][:, 0:1],
                     jnp.sum(q32[HEAD_DIM:], axis=0, keepdims=True)
                     * kmax2[st][:, HEAD_DIM:HEAD_DIM + 1]], axis=1)
                best = jnp.maximum(best, prod)
            return best

        top = lax.fori_loop(0, qall_ref.shape[2] // NORM_CHUNK, query_chunk,
                            jnp.zeros((1, 2 * NORM_CHUNK), F32))
        spread = 2.04 * jnp.sqrt(jnp.max(top, axis=1, keepdims=True))
        reach = (spread + SKIP_MARGIN) / c
        pairs = jnp.floor(reach * (1.0 / (2 * K_TILE))) + 1.0
        ctrl_ref[0] = jnp.clip(pairs, 0.0, float(2 ** 20)).astype(jnp.int32)[0, 0]
        ctrl_ref[1] = jnp.where(spread < FIXED_REFERENCE_SPREAD, 1, 0)[0, 0]

    ones_rows = jnp.ones((SUM_ROWS, K_TILE), BF16)

    def query_matrix(st):
        qt = qt_ref[st]
        feat = lax.broadcasted_iota(jnp.int32, qt.shape, 0)
        zero = jnp.zeros_like(qt)
        return jnp.concatenate(
            [jnp.concatenate([jnp.where(feat < HEAD_DIM, qt, zero),
                              jnp.where(feat >= HEAD_DIM, qt, zero)], axis=1),
             qext_ref[...]], axis=0)

    def query_bounds(st):
        q32 = qt_ref[st].astype(F32)
        q32 = q32 * q32
        qn2 = jnp.concatenate([jnp.sum(q32[:HEAD_DIM], axis=0, keepdims=True),
                               jnp.sum(q32[HEAD_DIM:], axis=0, keepdims=True)], axis=1)
        return 1.02 * jnp.sqrt(qn2 * kmax2_ref[st])

    n_pairs = jnp.minimum(qi, ctrl_ref[0])
    fixed_reference_ok = ctrl_ref[1] == 1
    first = qi * DIAG_TILES

    def key_rows(st, kt):
        ks = pl.multiple_of(kt * K_TILE, K_TILE)
        return jnp.concatenate([k_ref[st, pl.ds(ks, K_TILE), :], kext_ref[...]], axis=1)

    def value_rows(st, kt):
        ks = pl.multiple_of(kt * K_TILE, K_TILE)
        return jnp.concatenate([vt_ref[st, :, pl.ds(ks, K_TILE)], ones_rows], axis=0)

    def tile_offset(kt):
        return c * (kt * K_TILE - qi * Q_TILE).astype(F32)

    @pl.when(fixed_reference_ok)
    def _():
        rhs = [query_matrix(st) for st in streams]
        bounds = [query_bounds(st) for st in streams]

        def tiles(t0, count, diagonal=False):
            for st in streams:
                probs = []
                for i in range(count):
                    s = jnp.dot(key_rows(st, t0 - i), rhs[st], preferred_element_type=F32)
                    if diagonal:
                        s = s + mask_ref[count - 1 - i]
                    probs.append(jnp.exp2(s + (bounds[st] + tile_offset(t0 - i))).astype(BF16))
                values = jnp.concatenate([value_rows(st, t0 - i) for i in range(count)], axis=1)
                part = jnp.dot(values, jnp.concatenate(probs, axis=0), preferred_element_type=F32)
                acc_ref[st] = part if diagonal else acc_ref[st] + part

        tiles(first + DIAG_TILES - 1, DIAG_TILES, diagonal=True)
        n_quads = n_pairs // 2

        def quad(j, carry):
            tiles(first - 1 - 4 * j, 4)
            return carry

        lax.fori_loop(0, n_quads, quad, 0)

        @pl.when(n_pairs % 2 == 1)
        def _():
            tiles(first - 1 - 4 * n_quads, 2)

    @pl.when(jnp.logical_not(fixed_reference_ok))
    def _():
        rhs = [query_matrix(st) for st in streams]
        acc_ref[...] = jnp.zeros_like(acc_ref)

        def scores(st, kt, buf, diag=None):
            s = jnp.dot(key_rows(st, kt), rhs[st], preferred_element_type=F32)
            if diag is not None:
                s = s + mask_ref[diag]
            s_ref[st, buf] = s
            return jnp.max(s, axis=0, keepdims=True)

        def update(st, kt, buf, smax, m):
            off = tile_offset(kt)
            m_new = jnp.maximum(m, smax + off)
            alpha = jnp.exp2(m - m_new)
            p = jnp.exp2(s_ref[st, buf] - (m_new - off))
            acc_ref[st] = alpha * acc_ref[st] + jnp.dot(value_rows(st, kt), p.astype(BF16),
                                                        preferred_element_type=F32)
            return m_new

        m = [jnp.full((1, 2 * Q_TILE), -1e30, F32) for _ in streams]
        smax_b = [scores(st, first + 1, 1, diag=1) for st in streams]
        smax_a = [scores(st, first, 0, diag=0) for st in streams]
        m = [update(st, first + 1, 1, smax_b[st], m[st]) for st in streams]

        def pair(j, carry):
            smax_a, m = carry
            t = first - 2 * j
            smax_b = [scores(st, t - 1, 1) for st in streams]
            m = [update(st, t, 0, smax_a[st], m[st]) for st in streams]
            smax_a = [scores(st, t - 2, 0) for st in streams]
            m = [update(st, t - 1, 1, smax_b[st], m[st]) for st in streams]
            return smax_a, m

        smax_a, m = lax.fori_loop(0, n_pairs, pair, (smax_a, m))
        for st in streams:
            update(st, first - 2 * n_pairs, 0, smax_a[st], m[st])

    lv = lam_ref[...]
    lam = (jnp.exp(jnp.sum(lv[0:1] * lv[1:2], axis=-1, keepdims=True))
           - jnp.exp(jnp.sum(lv[2:3] * lv[3:4], axis=-1, keepdims=True)) + LAMBDA_INIT)
    for st in streams:
        acc = acc_ref[st]
        num = acc[:V_DIM]
        den = acc[V_DIM:V_DIM + 1]
        o = (num[:, :Q_TILE] / den[:, :Q_TILE] - lam * (num[:, Q_TILE:] / den[:, Q_TILE:]))
        o = o * lax.rsqrt(jnp.mean(o * o, axis=0, keepdims=True) + EPS) * sub_ref[...]
        o_ref[st] = (o * (1.0 - LAMBDA_INIT)).astype(BF16)


def _alibi_coefficients():
    pieces = []
    rest = np.float64(LOG2E)
    for _ in range(3):
        piece = np.float64(np.asarray(rest, np.float32).astype(jnp.bfloat16).astype(np.float32))
        pieces.append(piece)
        rest = rest - piece
    rows = []
    for h in range(N_HEADS):
        slope = 2.0 ** (-8.0 * (h + 1) / N_HEADS)
        rows.append([slope * p for p in pieces] + [slope * sum(pieces)])
    return jnp.asarray(np.array(rows, dtype=np.float32))


def _diff_attention(lam_vecs, qt, k, vt, subln):
    b, s, _ = k.shape
    return pl.pallas_call(
        _attn_kernel,
        grid=(b // BATCH_BLOCK, N_HEADS, s // Q_TILE),
        in_specs=[
            pl.BlockSpec(memory_space=pltpu.SMEM),
            pl.BlockSpec((4, HEAD_DIM), lambda bi, hi, qi: (0, 0)),
            pl.BlockSpec((BATCH_BLOCK, V_DIM, Q_TILE), lambda bi, hi, qi: (bi, hi, qi)),
            pl.BlockSpec((BATCH_BLOCK, V_DIM, s), lambda bi, hi, qi: (bi, hi, 0)),
            pl.BlockSpec((BATCH_BLOCK, s, V_DIM), lambda bi, hi, qi: (bi, 0, hi)),
            pl.BlockSpec((BATCH_BLOCK, V_DIM, s), lambda bi, hi, qi: (bi, hi, 0)),
            pl.BlockSpec((V_DIM, Q_TILE), lambda bi, hi, qi: (0, 0)),
        ],
        out_specs=pl.BlockSpec((BATCH_BLOCK, V_DIM, Q_TILE), lambda bi, hi, qi: (bi, hi, qi)),
        out_shape=jax.ShapeDtypeStruct((b, N_HEADS * V_DIM, s), BF16),
        scratch_shapes=[
            pltpu.VMEM((K_TILE, 128), BF16),
            pltpu.VMEM((128, 2 * Q_TILE), BF16),
            pltpu.VMEM((DIAG_TILES, K_TILE, 2 * Q_TILE), F32),
            pltpu.VMEM((BATCH_BLOCK, 1, 2 * Q_TILE), F32),
            pltpu.VMEM((BATCH_BLOCK, 2, K_TILE, 2 * Q_TILE), F32),
            pltpu.VMEM((BATCH_BLOCK, V_DIM + SUM_ROWS, 2 * Q_TILE), F32),
            pltpu.SMEM((2,), jnp.int32),
        ],
        compiler_params=pltpu.CompilerParams(
            dimension_semantics=("arbitrary", "arbitrary", "arbitrary"),
            vmem_limit_bytes=V7X_VMEM_LIMIT_BYTES),
        name="diff_attention",
    )(_alibi_coefficients(), lam_vecs, qt, qt, k, vt, subln)


def kernel(x, a_norm, a_w_in, a_v_norm, a_w_sp, a_b_sp, a_w_out, ffn_norm, ffn_w_gu, ffn_w_down,
           kv_norm, kv_w, b_norm, b_w_q, b_lambda, b_subln, b_w_o, final_norm):
    b, s, d = x.shape
    assert d == D_MODEL and s % Q_TILE == 0 and s % ROW_TILE == 0
    assert DIAG_TILES == 2 and K_TILE <= BF16_EXACT_INT and b % BATCH_BLOCK == 0
    xs = x.reshape(b * s, d)
    row = lambda g: g.reshape(1, -1).astype(F32)

    b_full = jnp.repeat(a_b_sp[0].T.astype(F32), GROUP_DIM, axis=1)
    xs = _gmlp_layer(xs, row(a_norm[0]), a_w_in[0].astype(BF16), row(a_v_norm[0]),
                     a_w_sp[0].astype(F32), b_full, a_w_out[0].astype(BF16))

    w_k = kv_w[:, :D_MODEL].astype(BF16)
    w_vt = kv_w[:, D_MODEL:].T.astype(BF16)
    w_qt = b_w_q[0].T.astype(BF16)
    xs, k, qt, vt = _ffn_qkv_layer(xs, b, row(ffn_norm[0]), ffn_w_gu[0].astype(BF16),
                                   ffn_w_down[0].astype(BF16), row(kv_norm), w_k, w_vt,
                                   row(b_norm[0]), w_qt)

    sub_cols = jnp.broadcast_to(b_subln[0].astype(F32)[:, None], (V_DIM, Q_TILE))
    attn = _diff_attention(b_lambda[0].astype(F32), qt, k.reshape(b, s, d), vt, sub_cols)

    out = _oproj_ffn_final_layer(xs, attn, b_w_o[0].astype(BF16),
                                 row(ffn_norm[1]), ffn_w_gu[1].astype(BF16),
                                 ffn_w_down[1].astype(BF16), row(final_norm))
    return out.reshape(b, s, d)
```

```python
import math

import jax
import jax.numpy as jnp
import numpy as np
from jax import lax
from jax.experimental import pallas as pl
from jax.experimental.pallas import tpu as pltpu

D_MODEL = 1024
CHUNK = 128
N_GROUPS = 8
GROUP_DIM = D_MODEL // N_GROUPS
N_HEADS = 8
HEAD_DIM = 64
V_DIM = 2 * HEAD_DIM
D_FF = 2816
EPS = 1e-6
N_A = 1
LAMBDA_INIT = 0.8 - 0.6 * math.exp(-0.3 * N_A)
LOG2E = math.log2(math.e)

V7X_VMEM_LIMIT_BYTES = 56 * 1024 * 1024
BF16_EXACT_INT = 256
BF16_SUBLANES = 16

ROW_TILE = 512
Q_TILE = 512
K_TILE = 256
DIAG_TILES = Q_TILE // K_TILE
SUM_ROWS = BF16_SUBLANES
NORM_CHUNK = 1024
PV_TILES = 4
BATCH_BLOCK = 2
SKIP_MARGIN = 160.0
FIXED_REFERENCE_SPREAD = 100.0

F32 = jnp.float32
BF16 = jnp.bfloat16


def _rms_scale(x):
    return lax.rsqrt(jnp.mean(x * x, axis=-1, keepdims=True) + EPS)


def _gelu_tanh(x):
    c = math.sqrt(2.0 / math.pi)
    a = -2.0 * c * LOG2E
    b = a * 0.044715
    return x / (1.0 + jnp.exp2(x * (a + b * (x * x))))


def _silu(x):
    return 0.5 * x * (1.0 + jnp.tanh(0.5 * x))


def _const_spec(shape):
    nd = len(shape)
    return pl.BlockSpec(shape, lambda *_: (0,) * nd, pipeline_mode=pl.Buffered(1))


def _row_spec(width):
    return pl.BlockSpec((ROW_TILE, width), lambda i: (i, 0))


def _dense_params():
    return pltpu.CompilerParams(dimension_semantics=("arbitrary",),
                                vmem_limit_bytes=V7X_VMEM_LIMIT_BYTES)


def _gmlp_kernel(x_ref, an_ref, win_ref, vg_ref, wsp_ref, bsp_ref, wout_ref, o_ref):
    t_idx = lax.broadcasted_iota(jnp.int32, (CHUNK, CHUNK), 0)
    s_idx = lax.broadcasted_iota(jnp.int32, (CHUNK, CHUNK), 1)
    causal = s_idx <= t_idx
    n_chunks = ROW_TILE // CHUNK

    x = x_ref[...]
    h = (x * _rms_scale(x) * an_ref[...]).astype(BF16)
    uv = _gelu_tanh(jnp.dot(h, win_ref[...], preferred_element_type=F32))
    u = uv[:, :D_MODEL]
    v = uv[:, D_MODEL:]
    mu = jnp.mean(v, axis=-1, keepdims=True)
    vc = v - mu
    var = jnp.mean(vc * vc, axis=-1, keepdims=True)
    vn = (vc * lax.rsqrt(var + EPS) * vg_ref[...]).astype(BF16)
    cols = []
    for g in range(N_GROUPS):
        wm = jnp.where(causal, wsp_ref[g], 0.0).astype(BF16)
        bias = bsp_ref[:, g * GROUP_DIM:(g + 1) * GROUP_DIM]
        vg = jnp.concatenate([vn[c * CHUNK:(c + 1) * CHUNK, g * GROUP_DIM:(g + 1) * GROUP_DIM]
                              for c in range(n_chunks)], axis=1)
        zg = jnp.dot(wm, vg, preferred_element_type=F32)
        cols.append(jnp.concatenate([zg[:, c * GROUP_DIM:(c + 1) * GROUP_DIM] + bias
                                     for c in range(n_chunks)], axis=0))
    z = jnp.concatenate(cols, axis=1)
    gated = (u * z).astype(BF16)
    o_ref[...] = x + jnp.dot(gated, wout_ref[...], preferred_element_type=F32)


def _gmlp_layer(x, a_norm, w_in, v_gain, w_sp, b_full, w_out):
    n = x.shape[0]
    return pl.pallas_call(
        _gmlp_kernel,
        grid=(n // ROW_TILE,),
        in_specs=[
            _row_spec(D_MODEL),
            _const_spec((1, D_MODEL)),
            _const_spec((D_MODEL, 2 * D_MODEL)),
            _const_spec((1, D_MODEL)),
            _const_spec((N_GROUPS, CHUNK, CHUNK)),
            _const_spec((CHUNK, D_MODEL)),
            _const_spec((D_MODEL, D_MODEL)),
        ],
        out_specs=_row_spec(D_MODEL),
        out_shape=jax.ShapeDtypeStruct((n, D_MODEL), F32),
        compiler_params=_dense_params(),
        name="gmlp_mixer",
    )(x, a_norm, w_in, v_gain, w_sp, b_full, w_out)


def _swiglu(x, fn, wgu_ref, wd_ref):
    h = (x * _rms_scale(x) * fn).astype(BF16)
    gu = jnp.dot(h, wgu_ref[...], preferred_element_type=F32)
    a = (_silu(gu[:, :D_FF]) * gu[:, D_FF:]).astype(BF16)
    return x + jnp.dot(a, wd_ref[...], preferred_element_type=F32)


def _ffn_qkv_kernel(x_ref, fn_ref, wgu_ref, wd_ref, kvn_ref, wk_ref, wvt_ref, qn_ref, wqt_ref,
                    x_out, k_out, qt_out, vt_out):
    y = _swiglu(x_ref[...], fn_ref[...], wgu_ref, wd_ref)
    x_out[...] = y
    yn = y * _rms_scale(y)
    hk = (yn * kvn_ref[...]).astype(BF16)
    hq = (yn * qn_ref[...]).astype(BF16)
    k_out[...] = jnp.dot(hk, wk_ref[...], preferred_element_type=F32).astype(BF16)
    nt = (((1,), (1,)), ((), ()))
    vt_out[0] = lax.dot_general(wvt_ref[...], hk, nt, preferred_element_type=F32).astype(BF16)
    qt = lax.dot_general(wqt_ref[...], hq, nt, preferred_element_type=F32)
    qt_out[0] = (qt * (HEAD_DIM ** -0.5 * LOG2E)).astype(BF16)


def _ffn_qkv_layer(x, batch, fn, w_gu, w_down, kv_norm, w_k, w_vt, q_norm, w_qt):
    n = x.shape[0]
    seq = n // batch
    tiles_per_seq = seq // ROW_TILE
    t_spec = pl.BlockSpec((1, D_MODEL, ROW_TILE),
                          lambda i: (i // tiles_per_seq, 0, i % tiles_per_seq))
    t_shape = jax.ShapeDtypeStruct((batch, D_MODEL, seq), BF16)
    return pl.pallas_call(
        _ffn_qkv_kernel,
        grid=(n // ROW_TILE,),
        in_specs=[
            _row_spec(D_MODEL),
            _const_spec((1, D_MODEL)),
            _const_spec((D_MODEL, 2 * D_FF)),
            _const_spec((D_FF, D_MODEL)),
            _const_spec((1, D_MODEL)),
            _const_spec((D_MODEL, D_MODEL)),
            _const_spec((D_MODEL, D_MODEL)),
            _const_spec((1, D_MODEL)),
            _const_spec((D_MODEL, D_MODEL)),
        ],
        out_specs=[_row_spec(D_MODEL), _row_spec(D_MODEL), t_spec, t_spec],
        out_shape=[jax.ShapeDtypeStruct((n, D_MODEL), F32),
                   jax.ShapeDtypeStruct((n, D_MODEL), BF16), t_shape, t_shape],
        compiler_params=_dense_params(),
        name="ffn0_qkv",
    )(x, fn, w_gu, w_down, kv_norm, w_k, w_vt, q_norm, w_qt)


def _oproj_ffn_final_kernel(x_ref, a_ref, wo_ref, fn_ref, wgu_ref, wd_ref, gn_ref, o_ref):
    x = x_ref[...] + lax.dot_general(a_ref[0], wo_ref[...], (((0,), (0,)), ((), ())),
                                     preferred_element_type=F32)
    y = _swiglu(x, fn_ref[...], wgu_ref, wd_ref)
    o_ref[...] = y * _rms_scale(y) * gn_ref[...]


def _oproj_ffn_final_layer(x, attn_t, w_o, fn, w_gu, w_down, final_norm):
    n = x.shape[0]
    tiles_per_seq = attn_t.shape[2] // ROW_TILE
    return pl.pallas_call(
        _oproj_ffn_final_kernel,
        grid=(n // ROW_TILE,),
        in_specs=[
            _row_spec(D_MODEL),
            pl.BlockSpec((1, D_MODEL, ROW_TILE),
                         lambda i: (i // tiles_per_seq, 0, i % tiles_per_seq)),
            _const_spec((D_MODEL, D_MODEL)),
            _const_spec((1, D_MODEL)),
            _const_spec((D_MODEL, 2 * D_FF)),
            _const_spec((D_FF, D_MODEL)),
            _const_spec((1, D_MODEL)),
        ],
        out_specs=_row_spec(D_MODEL),
        out_shape=jax.ShapeDtypeStruct((n, D_MODEL), F32),
        compiler_params=_dense_params(),
        name="oproj_ffn1_final",
    )(x, attn_t, w_o, fn, w_gu, w_down, final_norm)


def _attn_kernel(coef_ref, lam_ref, qt_ref, qall_ref, k_ref, vt_ref, sub_ref, o_ref,
                 kext_ref, qext_ref, mask_ref, kmax2_ref, lamrow_ref, s_ref, acc_ref, ctrl_ref):
    head = pl.program_id(1)
    qi = pl.program_id(2)
    streams = range(BATCH_BLOCK)
    c_parts = [coef_ref[head, i] for i in range(3)]
    c = coef_ref[head, 3]

    @pl.when(qi == 0)
    def _():
        lane = lax.broadcasted_iota(jnp.int32, (K_TILE, 128), 1)
        key_pos = lax.broadcasted_iota(jnp.int32, (K_TILE, 128), 0).astype(F32)
        kext = jnp.where(lane < 3, key_pos, 0.0)
        row = lax.broadcasted_iota(jnp.int32, (128, 2 * Q_TILE), 0)
        qry = lax.broadcasted_iota(jnp.int32, (128, 2 * Q_TILE), 1) % Q_TILE
        q_lo = -(qry % BF16_EXACT_INT).astype(F32)
        q_hi = -(qry - qry % BF16_EXACT_INT).astype(F32)
        qext = jnp.where((row >= 3) & (row < 6), q_lo, jnp.where((row >= 6) & (row < 9), q_hi, 0.0))
        for i in range(3):
            kext = jnp.where((lane == 3 + i) | (lane == 6 + i), c_parts[i], kext)
            qext = jnp.where(row == i, c_parts[i], qext)
        kext_ref[...] = kext.astype(BF16)
        qext_ref[...] = qext.astype(BF16)
        key = lax.broadcasted_iota(jnp.int32, (K_TILE, Q_TILE), 0)
        qcol = lax.broadcasted_iota(jnp.int32, (K_TILE, Q_TILE), 1)
        for d in range(DIAG_TILES):
            hide = jnp.where(key + d * K_TILE <= qcol, 0.0, -jnp.inf)
            mask_ref[d] = jnp.concatenate([hide, hide], axis=1)
        lv = lam_ref[...]
        lam = (jnp.exp(jnp.sum(lv[0:1] * lv[1:2], axis=-1, keepdims=True))
               - jnp.exp(jnp.sum(lv[2:3] * lv[3:4], axis=-1, keepdims=True)) + LAMBDA_INIT)
        lamrow_ref[...] = jnp.broadcast_to(lam, (1, Q_TILE))

        same_half = ((lax.broadcasted_iota(jnp.int32, (V_DIM, V_DIM), 0) < HEAD_DIM)
                     == (lax.broadcasted_iota(jnp.int32, (V_DIM, V_DIM), 1) < HEAD_DIM))
        selector = jnp.where(same_half, 1.0, 0.0).astype(BF16)

        def norm_chunk(i, best):
            start = pl.multiple_of(i * NORM_CHUNK, NORM_CHUNK)
            out = []
            for st in streams:
                rows = k_ref[st, pl.ds(start, NORM_CHUNK), :]
                sq = jnp.dot(rows * rows, selector, preferred_element_type=F32)
                out.append(jnp.maximum(best[st], jnp.max(sq, axis=0, keepdims=True)))
            return tuple(out)

        kmax2 = lax.fori_loop(0, k_ref.shape[1] // NORM_CHUNK, norm_chunk,
                              tuple(jnp.zeros((1, V_DIM), F32) for _ in streams), unroll=2)
        for st in streams:
            kmax2_ref[st] = jnp.concatenate(
                [jnp.broadcast_to(kmax2[st][:, 0:1], (1, Q_TILE)),
                 jnp.broadcast_to(kmax2[st][:, HEAD_DIM:HEAD_DIM + 1], (1, Q_TILE))], axis=1)

        def query_chunk(i, best):
            start = pl.multiple_of(i * NORM_CHUNK, NORM_CHUNK)
            for st in streams:
                q32 = qall_ref[st, :, pl.ds(start, NORM_CHUNK)].astype(F32)
                q32 = q32 * q32
                prod = jnp.concatenate(
                    [jnp.sum(q32[:HEAD_DIM], axis=0, keepdims=True) * kmax2[st][:, 0:1],
                     jnp.sum(q32[HEAD_DIM:], axis=0, keepdims=True)
                     * kmax2[st][:, HEAD_DIM:HEAD_DIM + 1]], axis=1)
                best = jnp.maximum(best, prod)
            return best

        top = lax.fori_loop(0, qall_ref.shape[2] // NORM_CHUNK, query_chunk,
                            jnp.zeros((1, 2 * NORM_CHUNK), F32))
        spread = 2.04 * jnp.sqrt(jnp.max(top, axis=1, keepdims=True))
        reach = (spread + SKIP_MARGIN) / c
        pairs = jnp.floor(reach * (1.0 / (2 * K_TILE))) + 1.0
        ctrl_ref[0] = jnp.clip(pairs, 0.0, float(2 ** 20)).astype(jnp.int32)[0, 0]
        ctrl_ref[1] = jnp.where(spread < FIXED_REFERENCE_SPREAD, 1, 0)[0, 0]

    ones_rows = jnp.ones((SUM_ROWS, K_TILE), BF16)

    def query_matrix(st):
        qt = qt_ref[st]
        feat = lax.broadcasted_iota(jnp.int32, qt.shape, 0)
        zero = jnp.zeros_like(qt)
        return jnp.concatenate(
            [jnp.concatenate([jnp.where(feat < HEAD_DIM, qt, zero),
                              jnp.where(feat >= HEAD_DIM, qt, zero)], axis=1),
             qext_ref[...]], axis=0)

    def query_bounds(st):
        q32 = qt_ref[st].astype(F32)
        q32 = q32 * q32
        qn2 = jnp.concatenate([jnp.sum(q32[:HEAD_DIM], axis=0, keepdims=True),
                               jnp.sum(q32[HEAD_DIM:], axis=0, keepdims=True)], axis=1)
        return 1.02 * jnp.sqrt(qn2 * kmax2_ref[st])

    n_pairs = jnp.minimum(qi, ctrl_ref[0])
    fixed_reference_ok = ctrl_ref[1] == 1
    first = qi * DIAG_TILES

    def key_rows(st, kt):
        ks = pl.multiple_of(kt * K_TILE, K_TILE)
        return jnp.concatenate([k_ref[st, pl.ds(ks, K_TILE), :], kext_ref[...]], axis=1)

    def value_rows(st, kt):
        ks = pl.multiple_of(kt * K_TILE, K_TILE)
        return jnp.concatenate([vt_ref[st, :, pl.ds(ks, K_TILE)], ones_rows], axis=0)

    def tile_offset(kt):
        return c * (kt * K_TILE - qi * Q_TILE).astype(F32)

    @pl.when(fixed_reference_ok)
    def _():
        rhs = [query_matrix(st) for st in streams]
        bounds = [query_bounds(st) for st in streams]

        def tiles(t0, count, diagonal=False):
            kts, offs = [], []
            for i in range(count):
                if diagonal and i >= DIAG_TILES:
                    kts.append(jnp.maximum(t0 - i, 0))
                    offs.append(jnp.where(n_pairs > 0, tile_offset(kts[i]), -jnp.inf))
                else:
                    kts.append(t0 - i)
                    offs.append(tile_offset(kts[i]))
            for st in streams:
                probs = []
                for i in range(count):
                    s = jnp.dot(key_rows(st, kts[i]), rhs[st], preferred_element_type=F32)
                    if diagonal and i < DIAG_TILES:
                        s = s + mask_ref[DIAG_TILES - 1 - i]
                    probs.append(jnp.exp2(s + (bounds[st] + offs[i])).astype(BF16))
                part = None
                for g in range(0, count, PV_TILES):
                    group = range(g, min(g + PV_TILES, count))
                    values = jnp.concatenate([value_rows(st, kts[i]) for i in group], axis=1)
                    prod = jnp.dot(values, jnp.concatenate([probs[i] for i in group], axis=0),
                                   preferred_element_type=F32)
                    part = prod if part is None else part + prod
                acc_ref[st] = part if diagonal else acc_ref[st] + part

        tiles(first + DIAG_TILES - 1, DIAG_TILES + 2, diagonal=True)
        far_pairs = jnp.maximum(n_pairs - 1, 0)
        n_quads = far_pairs // 2
        far_start = first - 3

        def quad(j, carry):
            tiles(far_start - 4 * j, 4)
            return carry

        lax.fori_loop(0, n_quads, quad, 0)

        @pl.when(far_pairs % 2 == 1)
        def _():
            tiles(far_start - 4 * n_quads, 2)

    @pl.when(jnp.logical_not(fixed_reference_ok))
    def _():
        rhs = [query_matrix(st) for st in streams]
        acc_ref[...] = jnp.zeros_like(acc_ref)

        def scores(st, kt, buf, diag=None):
            s = jnp.dot(key_rows(st, kt), rhs[st], preferred_element_type=F32)
            if diag is not None:
                s = s + mask_ref[diag]
            s_ref[st, buf] = s
            return jnp.max(s, axis=0, keepdims=True)

        def update(st, kt, buf, smax, m):
            off = tile_offset(kt)
            m_new = jnp.maximum(m, smax + off)
            alpha = jnp.exp2(m - m_new)
            p = jnp.exp2(s_ref[st, buf] - (m_new - off))
            acc_ref[st] = alpha * acc_ref[st] + jnp.dot(value_rows(st, kt), p.astype(BF16),
                                                        preferred_element_type=F32)
            return m_new

        m = [jnp.full((1, 2 * Q_TILE), -1e30, F32) for _ in streams]
        smax_b = [scores(st, first + 1, 1, diag=1) for st in streams]
        smax_a = [scores(st, first, 0, diag=0) for st in streams]
        m = [update(st, first + 1, 1, smax_b[st], m[st]) for st in streams]

        def pair(j, carry):
            smax_a, m = carry
            t = first - 2 * j
            smax_b = [scores(st, t - 1, 1) for st in streams]
            m = [update(st, t, 0, smax_a[st], m[st]) for st in streams]
            smax_a = [scores(st, t - 2, 0) for st in streams]
            m = [update(st, t - 1, 1, smax_b[st], m[st]) for st in streams]
            return smax_a, m

        smax_a, m = lax.fori_loop(0, n_pairs, pair, (smax_a, m))
        for st in streams:
            update(st, first - 2 * n_pairs, 0, smax_a[st], m[st])

    lam = lamrow_ref[...]
    for st in streams:
        acc = acc_ref[st]
        inv = 1.0 / acc[V_DIM:V_DIM + 1]
        o = (acc[:V_DIM, :Q_TILE] * inv[:, :Q_TILE]
             - acc[:V_DIM, Q_TILE:] * (lam * inv[:, Q_TILE:]))
        norm = lax.rsqrt(jnp.mean(o * o, axis=0, keepdims=True) + EPS) * (1.0 - LAMBDA_INIT)
        o_ref[st] = (o * norm * sub_ref[...]).astype(BF16)


def _alibi_coefficients():
    pieces = []
    rest = np.float64(LOG2E)
    for _ in range(3):
        piece = np.float64(np.asarray(rest, np.float32).astype(jnp.bfloat16).astype(np.float32))
        pieces.append(piece)
        rest = rest - piece
    rows = []
    for h in range(N_HEADS):
        slope = 2.0 ** (-8.0 * (h + 1) / N_HEADS)
        rows.append([slope * p for p in pieces] + [slope * sum(pieces)])
    return jnp.asarray(np.array(rows, dtype=np.float32))


def _diff_attention(lam_vecs, qt, k, vt, subln):
    b, s, _ = k.shape
    return pl.pallas_call(
        _attn_kernel,
        grid=(b // BATCH_BLOCK, N_HEADS, s // Q_TILE),
        in_specs=[
            pl.BlockSpec(memory_space=pltpu.SMEM),
            pl.BlockSpec((4, HEAD_DIM), lambda bi, hi, qi: (0, 0)),
            pl.BlockSpec((BATCH_BLOCK, V_DIM, Q_TILE), lambda bi, hi, qi: (bi, hi, qi)),
            pl.BlockSpec((BATCH_BLOCK, V_DIM, s), lambda bi, hi, qi: (bi, hi, 0)),
            pl.BlockSpec((BATCH_BLOCK, s, V_DIM), lambda bi, hi, qi: (bi, 0, hi)),
            pl.BlockSpec((BATCH_BLOCK, V_DIM, s), lambda bi, hi, qi: (bi, hi, 0)),
            pl.BlockSpec((V_DIM, Q_TILE), lambda bi, hi, qi: (0, 0)),
        ],
        out_specs=pl.BlockSpec((BATCH_BLOCK, V_DIM, Q_TILE), lambda bi, hi, qi: (bi, hi, qi)),
        out_shape=jax.ShapeDtypeStruct((b, N_HEADS * V_DIM, s), BF16),
        scratch_shapes=[
            pltpu.VMEM((K_TILE, 128), BF16),
            pltpu.VMEM((128, 2 * Q_TILE), BF16),
            pltpu.VMEM((DIAG_TILES, K_TILE, 2 * Q_TILE), F32),
            pltpu.VMEM((BATCH_BLOCK, 1, 2 * Q_TILE), F32),
            pltpu.VMEM((1, Q_TILE), F32),
            pltpu.VMEM((BATCH_BLOCK, 2, K_TILE, 2 * Q_TILE), F32),
            pltpu.VMEM((BATCH_BLOCK, V_DIM + SUM_ROWS, 2 * Q_TILE), F32),
            pltpu.SMEM((2,), jnp.int32),
        ],
        compiler_params=pltpu.CompilerParams(
            dimension_semantics=("arbitrary", "arbitrary", "arbitrary"),
            vmem_limit_bytes=V7X_VMEM_LIMIT_BYTES),
        name="diff_attention",
    )(_alibi_coefficients(), lam_vecs, qt, qt, k, vt, subln)


def kernel(x, a_norm, a_w_in, a_v_norm, a_w_sp, a_b_sp, a_w_out, ffn_norm, ffn_w_gu, ffn_w_down,
           kv_norm, kv_w, b_norm, b_w_q, b_lambda, b_subln, b_w_o, final_norm):
    b, s, d = x.shape
    assert d == D_MODEL and s % Q_TILE == 0 and s % ROW_TILE == 0
    assert DIAG_TILES == 2 and K_TILE <= BF16_EXACT_INT and b % BATCH_BLOCK == 0
    xs = x.reshape(b * s, d)
    row = lambda g: g.reshape(1, -1).astype(F32)

    b_full = jnp.repeat(a_b_sp[0].T.astype(F32), GROUP_DIM, axis=1)
    xs = _gmlp_layer(xs, row(a_norm[0]), a_w_in[0].astype(BF16), row(a_v_norm[0]),
                     a_w_sp[0].astype(F32), b_full, a_w_out[0].astype(BF16))

    w_k = kv_w[:, :D_MODEL].astype(BF16)
    w_vt = kv_w[:, D_MODEL:].T.astype(BF16)
    w_qt = b_w_q[0].T.astype(BF16)
    xs, k, qt, vt = _ffn_qkv_layer(xs, b, row(ffn_norm[0]), ffn_w_gu[0].astype(BF16),
                                   ffn_w_down[0].astype(BF16), row(kv_norm), w_k, w_vt,
                                   row(b_norm[0]), w_qt)

    sub_cols = jnp.broadcast_to(b_subln[0].astype(F32)[:, None], (V_DIM, Q_TILE))
    attn = _diff_attention(b_lambda[0].astype(F32), qt, k.reshape(b, s, d), vt, sub_cols)

    out = _oproj_ffn_final_layer(xs, attn, b_w_o[0].astype(BF16),
                                 row(ffn_norm[1]), ffn_w_gu[1].astype(BF16),
                                 ffn_w_down[1].astype(BF16), row(final_norm))
    return out.reshape(b, s, d)
```

```python
import math

import jax
import jax.numpy as jnp
import numpy as np
from jax import lax
from jax.experimental import pallas as pl
from jax.experimental.pallas import tpu as pltpu

D_MODEL = 1024
CHUNK = 128
N_GROUPS = 8
GROUP_DIM = D_MODEL // N_GROUPS
N_HEADS = 8
HEAD_DIM = 64
V_DIM = 2 * HEAD_DIM
D_FF = 2816
EPS = 1e-6
N_A = 1
LAMBDA_INIT = 0.8 - 0.6 * math.exp(-0.3 * N_A)
LOG2E = math.log2(math.e)

V7X_VMEM_LIMIT_BYTES = 56 * 1024 * 1024
BF16_EXACT_INT = 256
BF16_SUBLANES = 16

ROW_TILE = 512
Q_TILE = 512
K_TILE = 256
DIAG_TILES = Q_TILE // K_TILE
SUM_PARTIALS = 8
NORM_CHUNK = 1024
PV_TILES = 4
BATCH_BLOCK = 2
SKIP_MARGIN = 160.0
FIXED_REFERENCE_SPREAD = 100.0

F32 = jnp.float32
BF16 = jnp.bfloat16


def _rms_scale(x):
    return lax.rsqrt(jnp.mean(x * x, axis=-1, keepdims=True) + EPS)


def _gelu_tanh(x):
    c = math.sqrt(2.0 / math.pi)
    return 0.5 * x * (1.0 + jnp.tanh(c * (x + 0.044715 * (x * x * x))))


def _silu(x):
    return 0.5 * x * (1.0 + jnp.tanh(0.5 * x))


def _const_spec(shape):
    nd = len(shape)
    return pl.BlockSpec(shape, lambda *_: (0,) * nd, pipeline_mode=pl.Buffered(1))


def _row_spec(width):
    return pl.BlockSpec((ROW_TILE, width), lambda i: (i, 0))


def _dense_params():
    return pltpu.CompilerParams(dimension_semantics=("arbitrary",),
                                vmem_limit_bytes=V7X_VMEM_LIMIT_BYTES)


def _gmlp_kernel(x_ref, an_ref, win_ref, vg_ref, wsp_ref, bsp_ref, wout_ref, o_ref):
    t_idx = lax.broadcasted_iota(jnp.int32, (CHUNK, CHUNK), 0)
    s_idx = lax.broadcasted_iota(jnp.int32, (CHUNK, CHUNK), 1)
    causal = s_idx <= t_idx
    n_chunks = ROW_TILE // CHUNK

    x = x_ref[...]
    h = (x * _rms_scale(x) * an_ref[...]).astype(BF16)
    uv = _gelu_tanh(jnp.dot(h, win_ref[...], preferred_element_type=F32))
    u = uv[:, :D_MODEL]
    v = uv[:, D_MODEL:]
    mu = jnp.mean(v, axis=-1, keepdims=True)
    vc = v - mu
    var = jnp.mean(vc * vc, axis=-1, keepdims=True)
    vn = (vc * lax.rsqrt(var + EPS) * vg_ref[...]).astype(BF16)
    cols = []
    for g in range(N_GROUPS):
        wm = jnp.where(causal, wsp_ref[g], 0.0).astype(BF16)
        bias = bsp_ref[:, g * GROUP_DIM:(g + 1) * GROUP_DIM]
        vg = jnp.concatenate([vn[c * CHUNK:(c + 1) * CHUNK, g * GROUP_DIM:(g + 1) * GROUP_DIM]
                              for c in range(n_chunks)], axis=1)
        zg = jnp.dot(wm, vg, preferred_element_type=F32)
        cols.append(jnp.concatenate([zg[:, c * GROUP_DIM:(c + 1) * GROUP_DIM] + bias
                                     for c in range(n_chunks)], axis=0))
    z = jnp.concatenate(cols, axis=1)
    gated = (u * z).astype(BF16)
    o_ref[...] = x + jnp.dot(gated, wout_ref[...], preferred_element_type=F32)


def _gmlp_layer(x, a_norm, w_in, v_gain, w_sp, b_full, w_out):
    n = x.shape[0]
    return pl.pallas_call(
        _gmlp_kernel,
        grid=(n // ROW_TILE,),
        in_specs=[
            _row_spec(D_MODEL),
            _const_spec((1, D_MODEL)),
            _const_spec((D_MODEL, 2 * D_MODEL)),
            _const_spec((1, D_MODEL)),
            _const_spec((N_GROUPS, CHUNK, CHUNK)),
            _const_spec((CHUNK, D_MODEL)),
            _const_spec((D_MODEL, D_MODEL)),
        ],
        out_specs=_row_spec(D_MODEL),
        out_shape=jax.ShapeDtypeStruct((n, D_MODEL), F32),
        compiler_params=_dense_params(),
        name="gmlp_mixer",
    )(x, a_norm, w_in, v_gain, w_sp, b_full, w_out)


def _swiglu(x, fn, wgu_ref, wd_ref):
    h = (x * _rms_scale(x) * fn).astype(BF16)
    gu = jnp.dot(h, wgu_ref[...], preferred_element_type=F32)
    a = (_silu(gu[:, :D_FF]) * gu[:, D_FF:]).astype(BF16)
    return x + jnp.dot(a, wd_ref[...], preferred_element_type=F32)


def _ffn_qkv_kernel(x_ref, fn_ref, wgu_ref, wd_ref, kvn_ref, wk_ref, wvt_ref, qn_ref, wqt_ref,
                    x_out, k_out, qt_out, vt_out):
    y = _swiglu(x_ref[...], fn_ref[...], wgu_ref, wd_ref)
    x_out[...] = y
    yn = y * _rms_scale(y)
    hk = (yn * kvn_ref[...]).astype(BF16)
    hq = (yn * qn_ref[...]).astype(BF16)
    k_out[...] = jnp.dot(hk, wk_ref[...], preferred_element_type=F32).astype(BF16)
    nt = (((1,), (1,)), ((), ()))
    vt_out[0] = lax.dot_general(wvt_ref[...], hk, nt, preferred_element_type=F32).astype(BF16)
    qt = lax.dot_general(wqt_ref[...], hq, nt, preferred_element_type=F32)
    qt_out[0] = (qt * (HEAD_DIM ** -0.5 * LOG2E)).astype(BF16)


def _ffn_qkv_layer(x, batch, fn, w_gu, w_down, kv_norm, w_k, w_vt, q_norm, w_qt):
    n = x.shape[0]
    seq = n // batch
    tiles_per_seq = seq // ROW_TILE
    t_spec = pl.BlockSpec((1, D_MODEL, ROW_TILE),
                          lambda i: (i // tiles_per_seq, 0, i % tiles_per_seq))
    t_shape = jax.ShapeDtypeStruct((batch, D_MODEL, seq), BF16)
    return pl.pallas_call(
        _ffn_qkv_kernel,
        grid=(n // ROW_TILE,),
        in_specs=[
            _row_spec(D_MODEL),
            _const_spec((1, D_MODEL)),
            _const_spec((D_MODEL, 2 * D_FF)),
            _const_spec((D_FF, D_MODEL)),
            _const_spec((1, D_MODEL)),
            _const_spec((D_MODEL, D_MODEL)),
            _const_spec((D_MODEL, D_MODEL)),
            _const_spec((1, D_MODEL)),
            _const_spec((D_MODEL, D_MODEL)),
        ],
        out_specs=[_row_spec(D_MODEL), _row_spec(D_MODEL), t_spec, t_spec],
        out_shape=[jax.ShapeDtypeStruct((n, D_MODEL), F32),
                   jax.ShapeDtypeStruct((n, D_MODEL), BF16), t_shape, t_shape],
        compiler_params=_dense_params(),
        name="ffn0_qkv",
    )(x, fn, w_gu, w_down, kv_norm, w_k, w_vt, q_norm, w_qt)


def _oproj_ffn_final_kernel(x_ref, a_ref, wo_ref, fn_ref, wgu_ref, wd_ref, gn_ref, o_ref):
    x = x_ref[...] + lax.dot_general(a_ref[0], wo_ref[...], (((0,), (0,)), ((), ())),
                                     preferred_element_type=F32)
    y = _swiglu(x, fn_ref[...], wgu_ref, wd_ref)
    o_ref[...] = y * _rms_scale(y) * gn_ref[...]


def _oproj_ffn_final_layer(x, attn_t, w_o, fn, w_gu, w_down, final_norm):
    n = x.shape[0]
    tiles_per_seq = attn_t.shape[2] // ROW_TILE
    return pl.pallas_call(
        _oproj_ffn_final_kernel,
        grid=(n // ROW_TILE,),
        in_specs=[
            _row_spec(D_MODEL),
            pl.BlockSpec((1, D_MODEL, ROW_TILE),
                         lambda i: (i // tiles_per_seq, 0, i % tiles_per_seq)),
            _const_spec((D_MODEL, D_MODEL)),
            _const_spec((1, D_MODEL)),
            _const_spec((D_MODEL, 2 * D_FF)),
            _const_spec((D_FF, D_MODEL)),
            _const_spec((1, D_MODEL)),
        ],
        out_specs=_row_spec(D_MODEL),
        out_shape=jax.ShapeDtypeStruct((n, D_MODEL), F32),
        compiler_params=_dense_params(),
        name="oproj_ffn1_final",
    )(x, attn_t, w_o, fn, w_gu, w_down, final_norm)


def _attn_kernel(coef_ref, lam_ref, qt_ref, qall_ref, k_ref, vt_ref, sub_ref, o_ref,
                 kext_ref, qext_ref, mask_ref, kmax2_ref, lamrow_ref, s_ref, acc_ref, ctrl_ref):
    head = pl.program_id(1)
    qi = pl.program_id(2)
    streams = range(BATCH_BLOCK)
    c_parts = [coef_ref[head, i] for i in range(3)]
    c = coef_ref[head, 3]

    @pl.when(qi == 0)
    def _():
        lane = lax.broadcasted_iota(jnp.int32, (K_TILE, 128), 1)
        key_pos = lax.broadcasted_iota(jnp.int32, (K_TILE, 128), 0).astype(F32)
        kext = jnp.where(lane < 3, key_pos, 0.0)
        row = lax.broadcasted_iota(jnp.int32, (128, 2 * Q_TILE), 0)
        qry = lax.broadcasted_iota(jnp.int32, (128, 2 * Q_TILE), 1) % Q_TILE
        q_lo = -(qry % BF16_EXACT_INT).astype(F32)
        q_hi = -(qry - qry % BF16_EXACT_INT).astype(F32)
        qext = jnp.where((row >= 3) & (row < 6), q_lo, jnp.where((row >= 6) & (row < 9), q_hi, 0.0))
        for i in range(3):
            kext = jnp.where((lane == 3 + i) | (lane == 6 + i), c_parts[i], kext)
            qext = jnp.where(row == i, c_parts[i], qext)
        kext_ref[...] = kext.astype(BF16)
        qext_ref[...] = qext.astype(BF16)
        key = lax.broadcasted_iota(jnp.int32, (K_TILE, Q_TILE), 0)
        qcol = lax.broadcasted_iota(jnp.int32, (K_TILE, Q_TILE), 1)
        for d in range(DIAG_TILES):
            hide = jnp.where(key + d * K_TILE <= qcol, 0.0, -jnp.inf)
            mask_ref[d] = jnp.concatenate([hide, hide], axis=1)
        lv = lam_ref[...]
        lam = (jnp.exp(jnp.sum(lv[0:1] * lv[1:2], axis=-1, keepdims=True))
               - jnp.exp(jnp.sum(lv[2:3] * lv[3:4], axis=-1, keepdims=True)) + LAMBDA_INIT)
        lamrow_ref[...] = jnp.broadcast_to(lam, (1, Q_TILE))

        same_half = ((lax.broadcasted_iota(jnp.int32, (V_DIM, V_DIM), 0) < HEAD_DIM)
                     == (lax.broadcasted_iota(jnp.int32, (V_DIM, V_DIM), 1) < HEAD_DIM))
        selector = jnp.where(same_half, 1.0, 0.0).astype(BF16)

        def norm_chunk(i, best):
            start = pl.multiple_of(i * NORM_CHUNK, NORM_CHUNK)
            out = []
            for st in streams:
                rows = k_ref[st, pl.ds(start, NORM_CHUNK), :]
                sq = jnp.dot(rows * rows, selector, preferred_element_type=F32)
                out.append(jnp.maximum(best[st], jnp.max(sq, axis=0, keepdims=True)))
            return tuple(out)

        kmax2 = lax.fori_loop(0, k_ref.shape[1] // NORM_CHUNK, norm_chunk,
                              tuple(jnp.zeros((1, V_DIM), F32) for _ in streams), unroll=2)
        for st in streams:
            kmax2_ref[st] = jnp.concatenate(
                [jnp.broadcast_to(kmax2[st][:, 0:1], (1, Q_TILE)),
                 jnp.broadcast_to(kmax2[st][:, HEAD_DIM:HEAD_DIM + 1], (1, Q_TILE))], axis=1)

        def query_chunk(i, best):
            start = pl.multiple_of(i * NORM_CHUNK, NORM_CHUNK)
            for st in streams:
                q32 = qall_ref[st, :, pl.ds(start, NORM_CHUNK)].astype(F32)
                q32 = q32 * q32
                prod = jnp.concatenate(
                    [jnp.sum(q32[:HEAD_DIM], axis=0, keepdims=True) * kmax2[st][:, 0:1],
                     jnp.sum(q32[HEAD_DIM:], axis=0, keepdims=True)
                     * kmax2[st][:, HEAD_DIM:HEAD_DIM + 1]], axis=1)
                best = jnp.maximum(best, prod)
            return best

        top = lax.fori_loop(0, qall_ref.shape[2] // NORM_CHUNK, query_chunk,
                            jnp.zeros((1, 2 * NORM_CHUNK), F32))
        spread = 2.04 * jnp.sqrt(jnp.max(top, axis=1, keepdims=True))
        reach = (spread + SKIP_MARGIN) / c
        pairs = jnp.floor(reach * (1.0 / (2 * K_TILE))) + 1.0
        ctrl_ref[0] = jnp.clip(pairs, 0.0, float(2 ** 20)).astype(jnp.int32)[0, 0]
        ctrl_ref[1] = jnp.where(spread < FIXED_REFERENCE_SPREAD, 1, 0)[0, 0]

    def query_matrix(st):
        qt = qt_ref[st]
        feat = lax.broadcasted_iota(jnp.int32, qt.shape, 0)
        zero = jnp.zeros_like(qt)
        return jnp.concatenate(
            [jnp.concatenate([jnp.where(feat < HEAD_DIM, qt, zero),
                              jnp.where(feat >= HEAD_DIM, qt, zero)], axis=1),
             qext_ref[...]], axis=0)

    def query_bounds(st):
        q32 = qt_ref[st].astype(F32)
        q32 = q32 * q32
        qn2 = jnp.concatenate([jnp.sum(q32[:HEAD_DIM], axis=0, keepdims=True),
                               jnp.sum(q32[HEAD_DIM:], axis=0, keepdims=True)], axis=1)
        return 1.02 * jnp.sqrt(qn2 * kmax2_ref[st])

    n_pairs = jnp.minimum(qi, ctrl_ref[0])
    fixed_reference_ok = ctrl_ref[1] == 1
    first = qi * DIAG_TILES

    def key_rows(st, kt):
        ks = pl.multiple_of(kt * K_TILE, K_TILE)
        return jnp.concatenate([k_ref[st, pl.ds(ks, K_TILE), :], kext_ref[...]], axis=1)

    def value_rows(st, kt):
        ks = pl.multiple_of(kt * K_TILE, K_TILE)
        return vt_ref[st, :, pl.ds(ks, K_TILE)]

    def tile_offset(kt):
        return c * (kt * K_TILE - qi * Q_TILE).astype(F32)

    @pl.when(fixed_reference_ok)
    def _():
        rhs = [query_matrix(st) for st in streams]
        bounds = [query_bounds(st) for st in streams]

        def tiles(t0, count, diagonal=False):
            kts, offs = [], []
            for i in range(count):
                if diagonal and i >= DIAG_TILES:
                    kts.append(jnp.maximum(t0 - i, 0))
                    offs.append(jnp.where(n_pairs > 0, tile_offset(kts[i]), -jnp.inf))
                else:
                    kts.append(t0 - i)
                    offs.append(tile_offset(kts[i]))
            los = [(DIAG_TILES - 1 - i) * K_TILE if diagonal and i < DIAG_TILES else 0
                   for i in range(count)]

            def narrow(a, lo):
                return a if lo == 0 else jnp.concatenate([a[:, lo:Q_TILE], a[:, Q_TILE + lo:]], axis=1)

            def widen(a, lo):
                if lo == 0:
                    return a
                gap = jnp.zeros((a.shape[0], lo), a.dtype)
                return jnp.concatenate([gap, a[:, :Q_TILE - lo], gap, a[:, Q_TILE - lo:]], axis=1)

            for st in streams:
                probs, row_sums = [], None
                for i in range(count):
                    s = jnp.dot(key_rows(st, kts[i]), narrow(rhs[st], los[i]),
                                preferred_element_type=F32)
                    if diagonal and i < DIAG_TILES:
                        s = s + narrow(mask_ref[DIAG_TILES - 1 - i], los[i])
                    p = jnp.exp2(s + (narrow(bounds[st], los[i]) + offs[i]))
                    folded = jnp.sum(p.reshape(K_TILE // SUM_PARTIALS, SUM_PARTIALS, p.shape[1]), axis=0)
                    folded = widen(folded, los[i])
                    row_sums = folded if row_sums is None else row_sums + folded
                    probs.append(p.astype(BF16))
                part = None
                whole = [i for i in range(count) if los[i] == 0]
                for g in range(0, len(whole), PV_TILES):
                    group = whole[g:g + PV_TILES]
                    values = jnp.concatenate([value_rows(st, kts[i]) for i in group], axis=1)
                    prod = jnp.dot(values, jnp.concatenate([probs[i] for i in group], axis=0),
                                   preferred_element_type=F32)
                    part = prod if part is None else part + prod
                for i in range(count):
                    if los[i] > 0:
                        prod = jnp.dot(value_rows(st, kts[i]), probs[i], preferred_element_type=F32)
                        part = part + widen(prod, los[i])
                if diagonal:
                    acc_ref[st, :V_DIM] = part
                    acc_ref[st, V_DIM:V_DIM + SUM_PARTIALS] = row_sums
                else:
                    acc_ref[st, :V_DIM] += part
                    acc_ref[st, V_DIM:V_DIM + SUM_PARTIALS] += row_sums

        tiles(first + DIAG_TILES - 1, DIAG_TILES + 2, diagonal=True)
        far_pairs = jnp.maximum(n_pairs - 1, 0)
        n_quads = far_pairs // 2
        far_start = first - 3

        def quad(j, carry):
            tiles(far_start - 4 * j, 4)
            return carry

        lax.fori_loop(0, n_quads, quad, 0)

        @pl.when(far_pairs % 2 == 1)
        def _():
            tiles(far_start - 4 * n_quads, 2)

    @pl.when(jnp.logical_not(fixed_reference_ok))
    def _():
        rhs = [query_matrix(st) for st in streams]
        acc_ref[...] = jnp.zeros_like(acc_ref)
        sum_rows = jnp.where(lax.broadcasted_iota(jnp.int32, (BF16_SUBLANES, K_TILE), 0) == 0,
                             1.0, 0.0).astype(BF16)

        def scores(st, kt, buf, diag=None):
            s = jnp.dot(key_rows(st, kt), rhs[st], preferred_element_type=F32)
            if diag is not None:
                s = s + mask_ref[diag]
            s_ref[st, buf] = s
            return jnp.max(s, axis=0, keepdims=True)

        def update(st, kt, buf, smax, m):
            off = tile_offset(kt)
            m_new = jnp.maximum(m, smax + off)
            alpha = jnp.exp2(m - m_new)
            p = jnp.exp2(s_ref[st, buf] - (m_new - off))
            lhs = jnp.concatenate([value_rows(st, kt), sum_rows], axis=0)
            acc_ref[st] = alpha * acc_ref[st] + jnp.dot(lhs, p.astype(BF16),
                                                        preferred_element_type=F32)
            return m_new

        m = [jnp.full((1, 2 * Q_TILE), -1e30, F32) for _ in streams]
        smax_b = [scores(st, first + 1, 1, diag=1) for st in streams]
        smax_a = [scores(st, first, 0, diag=0) for st in streams]
        m = [update(st, first + 1, 1, smax_b[st], m[st]) for st in streams]

        def pair(j, carry):
            smax_a, m = carry
            t = first - 2 * j
            smax_b = [scores(st, t - 1, 1) for st in streams]
            m = [update(st, t, 0, smax_a[st], m[st]) for st in streams]
            smax_a = [scores(st, t - 2, 0) for st in streams]
            m = [update(st, t - 1, 1, smax_b[st], m[st]) for st in streams]
            return smax_a, m

        smax_a, m = lax.fori_loop(0, n_pairs, pair, (smax_a, m))
        for st in streams:
            update(st, first - 2 * n_pairs, 0, smax_a[st], m[st])

    lam = lamrow_ref[...]
    for st in streams:
        acc = acc_ref[st, :V_DIM]
        sums = jnp.sum(acc_ref[st, V_DIM:V_DIM + SUM_PARTIALS], axis=0, keepdims=True)
        inv = 1.0 / sums
        o = (acc[:, :Q_TILE] * inv[:, :Q_TILE]
             - acc[:, Q_TILE:] * (lam * inv[:, Q_TILE:]))
        norm = lax.rsqrt(jnp.mean(o * o, axis=0, keepdims=True) + EPS) * (1.0 - LAMBDA_INIT)
        o_ref[st] = (o * norm * sub_ref[...]).astype(BF16)


def _alibi_coefficients():
    pieces = []
    rest = np.float64(LOG2E)
    for _ in range(3):
        piece = np.float64(np.asarray(rest, np.float32).astype(jnp.bfloat16).astype(np.float32))
        pieces.append(piece)
        rest = rest - piece
    rows = []
    for h in range(N_HEADS):
        slope = 2.0 ** (-8.0 * (h + 1) / N_HEADS)
        rows.append([slope * p for p in pieces] + [slope * sum(pieces)])
    return jnp.asarray(np.array(rows, dtype=np.float32))


def _diff_attention(lam_vecs, qt, k, vt, subln):
    b, s, _ = k.shape
    return pl.pallas_call(
        _attn_kernel,
        grid=(b // BATCH_BLOCK, N_HEADS, s // Q_TILE),
        in_specs=[
            pl.BlockSpec(memory_space=pltpu.SMEM),
            pl.BlockSpec((4, HEAD_DIM), lambda bi, hi, qi: (0, 0)),
            pl.BlockSpec((BATCH_BLOCK, V_DIM, Q_TILE), lambda bi, hi, qi: (bi, hi, qi)),
            pl.BlockSpec((BATCH_BLOCK, V_DIM, s), lambda bi, hi, qi: (bi, hi, 0)),
            pl.BlockSpec((BATCH_BLOCK, s, V_DIM), lambda bi, hi, qi: (bi, 0, hi)),
            pl.BlockSpec((BATCH_BLOCK, V_DIM, s), lambda bi, hi, qi: (bi, hi, 0)),
            pl.BlockSpec((V_DIM, Q_TILE), lambda bi, hi, qi: (0, 0)),
        ],
        out_specs=pl.BlockSpec((BATCH_BLOCK, V_DIM, Q_TILE), lambda bi, hi, qi: (bi, hi, qi)),
        out_shape=jax.ShapeDtypeStruct((b, N_HEADS * V_DIM, s), BF16),
        scratch_shapes=[
            pltpu.VMEM((K_TILE, 128), BF16),
            pltpu.VMEM((128, 2 * Q_TILE), BF16),
            pltpu.VMEM((DIAG_TILES, K_TILE, 2 * Q_TILE), F32),
            pltpu.VMEM((BATCH_BLOCK, 1, 2 * Q_TILE), F32),
            pltpu.VMEM((1, Q_TILE), F32),
            pltpu.VMEM((BATCH_BLOCK, 2, K_TILE, 2 * Q_TILE), F32),
            pltpu.VMEM((BATCH_BLOCK, V_DIM + BF16_SUBLANES, 2 * Q_TILE), F32),
            pltpu.SMEM((2,), jnp.int32),
        ],
        compiler_params=pltpu.CompilerParams(
            dimension_semantics=("arbitrary", "arbitrary", "arbitrary"),
            vmem_limit_bytes=V7X_VMEM_LIMIT_BYTES),
        name="diff_attention",
    )(_alibi_coefficients(), lam_vecs, qt, qt, k, vt, subln)


def kernel(x, a_norm, a_w_in, a_v_norm, a_w_sp, a_b_sp, a_w_out, ffn_norm, ffn_w_gu, ffn_w_down,
           kv_norm, kv_w, b_norm, b_w_q, b_lambda, b_subln, b_w_o, final_norm):
    b, s, d = x.shape
    assert d == D_MODEL and s % Q_TILE == 0 and s % ROW_TILE == 0
    assert DIAG_TILES == 2 and K_TILE <= BF16_EXACT_INT and b % BATCH_BLOCK == 0
    xs = x.reshape(b * s, d)
    row = lambda g: g.reshape(1, -1).astype(F32)

    b_full = jnp.repeat(a_b_sp[0].T.astype(F32), GROUP_DIM, axis=1)
    xs = _gmlp_layer(xs, row(a_norm[0]), a_w_in[0].astype(BF16), row(a_v_norm[0]),
                     a_w_sp[0].astype(F32), b_full, a_w_out[0].astype(BF16))

    w_k = kv_w[:, :D_MODEL].astype(BF16)
    w_vt = kv_w[:, D_MODEL:].T.astype(BF16)
    w_qt = b_w_q[0].T.astype(BF16)
    xs, k, qt, vt = _ffn_qkv_layer(xs, b, row(ffn_norm[0]), ffn_w_gu[0].astype(BF16),
                                   ffn_w_down[0].astype(BF16), row(kv_norm), w_k, w_vt,
                                   row(b_norm[0]), w_qt)

    sub_cols = jnp.broadcast_to(b_subln[0].astype(F32)[:, None], (V_DIM, Q_TILE))
    attn = _diff_attention(b_lambda[0].astype(F32), qt, k.reshape(b, s, d), vt, sub_cols)

    out = _oproj_ffn_final_layer(xs, attn, b_w_o[0].astype(BF16),
                                 row(ffn_norm[1]), ffn_w_gu[1].astype(BF16),
                                 ffn_w_down[1].astype(BF16), row(final_norm))
    return out.reshape(b, s, d)
```

```python
import math

import jax
import jax.numpy as jnp
import numpy as np
from jax import lax
from jax.experimental import pallas as pl
from jax.experimental.pallas import tpu as pltpu

D_MODEL = 1024
CHUNK = 128
N_GROUPS = 8
GROUP_DIM = D_MODEL // N_GROUPS
N_HEADS = 8
HEAD_DIM = 64
V_DIM = 2 * HEAD_DIM
D_FF = 2816
EPS = 1e-6
N_A = 1
LAMBDA_INIT = 0.8 - 0.6 * math.exp(-0.3 * N_A)
LOG2E = math.log2(math.e)

V7X_VMEM_LIMIT_BYTES = 56 * 1024 * 1024
BF16_EXACT_INT = 256
BF16_SUBLANES = 16

ROW_TILE = 512
Q_TILE = 512
K_TILE = 256
DIAG_TILES = Q_TILE // K_TILE
SUM_PARTIALS = 8
NORM_CHUNK = 1024
KN_ROWS = 8
PV_TILES = 4
BATCH_BLOCK = 2
SKIP_MARGIN = 160.0
FIXED_REFERENCE_SPREAD = 100.0

F32 = jnp.float32
BF16 = jnp.bfloat16


def _rms_scale(x):
    return lax.rsqrt(jnp.mean(x * x, axis=-1, keepdims=True) + EPS)


def _gelu_tanh(x):
    c = math.sqrt(2.0 / math.pi)
    return 0.5 * x * (1.0 + jnp.tanh(c * (x + 0.044715 * (x * x * x))))


def _silu(x):
    return 0.5 * x * (1.0 + jnp.tanh(0.5 * x))


def _const_spec(shape):
    nd = len(shape)
    return pl.BlockSpec(shape, lambda *_: (0,) * nd, pipeline_mode=pl.Buffered(1))


def _row_spec(width):
    return pl.BlockSpec((ROW_TILE, width), lambda i: (i, 0))


def _dense_params():
    return pltpu.CompilerParams(dimension_semantics=("arbitrary",),
                                vmem_limit_bytes=V7X_VMEM_LIMIT_BYTES)


def _gmlp_kernel(x_ref, an_ref, win_ref, vg_ref, wsp_ref, bsp_ref, wout_ref, o_ref):
    t_idx = lax.broadcasted_iota(jnp.int32, (CHUNK, CHUNK), 0)
    s_idx = lax.broadcasted_iota(jnp.int32, (CHUNK, CHUNK), 1)
    causal = s_idx <= t_idx
    n_chunks = ROW_TILE // CHUNK

    x = x_ref[...]
    h = (x * _rms_scale(x) * an_ref[...]).astype(BF16)
    uv = _gelu_tanh(jnp.dot(h, win_ref[...], preferred_element_type=F32))
    u = uv[:, :D_MODEL]
    v = uv[:, D_MODEL:]
    mu = jnp.mean(v, axis=-1, keepdims=True)
    vc = v - mu
    var = jnp.mean(vc * vc, axis=-1, keepdims=True)
    vn = (vc * lax.rsqrt(var + EPS) * vg_ref[...]).astype(BF16)
    cols = []
    for g in range(N_GROUPS):
        wm = jnp.where(causal, wsp_ref[g], 0.0).astype(BF16)
        bias = bsp_ref[:, g * GROUP_DIM:(g + 1) * GROUP_DIM]
        vg = jnp.concatenate([vn[c * CHUNK:(c + 1) * CHUNK, g * GROUP_DIM:(g + 1) * GROUP_DIM]
                              for c in range(n_chunks)], axis=1)
        zg = jnp.dot(wm, vg, preferred_element_type=F32)
        cols.append(jnp.concatenate([zg[:, c * GROUP_DIM:(c + 1) * GROUP_DIM] + bias
                                     for c in range(n_chunks)], axis=0))
    z = jnp.concatenate(cols, axis=1)
    gated = (u * z).astype(BF16)
    o_ref[...] = x + jnp.dot(gated, wout_ref[...], preferred_element_type=F32)


def _gmlp_layer(x, a_norm, w_in, v_gain, w_sp, b_full, w_out):
    n = x.shape[0]
    return pl.pallas_call(
        _gmlp_kernel,
        grid=(n // ROW_TILE,),
        in_specs=[
            _row_spec(D_MODEL),
            _const_spec((1, D_MODEL)),
            _const_spec((D_MODEL, 2 * D_MODEL)),
            _const_spec((1, D_MODEL)),
            _const_spec((N_GROUPS, CHUNK, CHUNK)),
            _const_spec((CHUNK, D_MODEL)),
            _const_spec((D_MODEL, D_MODEL)),
        ],
        out_specs=_row_spec(D_MODEL),
        out_shape=jax.ShapeDtypeStruct((n, D_MODEL), F32),
        compiler_params=_dense_params(),
        name="gmlp_mixer",
    )(x, a_norm, w_in, v_gain, w_sp, b_full, w_out)


def _swiglu(x, fn, wgu_ref, wd_ref):
    h = (x * _rms_scale(x) * fn).astype(BF16)
    gu = jnp.dot(h, wgu_ref[...], preferred_element_type=F32)
    a = (_silu(gu[:, :D_FF]) * gu[:, D_FF:]).astype(BF16)
    return x + jnp.dot(a, wd_ref[...], preferred_element_type=F32)


def _ffn_qkv_kernel(x_ref, fn_ref, wgu_ref, wd_ref, kvn_ref, wk_ref, wvt_ref, qn_ref, wqt_ref,
                    x_out, k_out, qt_out, vt_out, kn_out):
    y = _swiglu(x_ref[...], fn_ref[...], wgu_ref, wd_ref)
    x_out[...] = y
    yn = y * _rms_scale(y)
    hk = (yn * kvn_ref[...]).astype(BF16)
    hq = (yn * qn_ref[...]).astype(BF16)
    kb = jnp.dot(hk, wk_ref[...], preferred_element_type=F32).astype(BF16)
    k_out[...] = kb
    nt = (((1,), (1,)), ((), ()))
    row = lax.broadcasted_iota(jnp.int32, (N_HEADS * KN_ROWS, D_MODEL), 0)
    col = lax.broadcasted_iota(jnp.int32, (N_HEADS * KN_ROWS, D_MODEL), 1)
    half = row % KN_ROWS
    pick = (half < 2) & (col // HEAD_DIM == 2 * (row // KN_ROWS) + half)
    sq = lax.dot_general(jnp.where(pick, 1.0, 0.0).astype(BF16), kb * kb, nt,
                         preferred_element_type=F32)
    kn_out[0] = sq.reshape(N_HEADS, KN_ROWS, sq.shape[1])
    vt_out[0] = lax.dot_general(wvt_ref[...], hk, nt, preferred_element_type=F32).astype(BF16)
    qt = lax.dot_general(wqt_ref[...], hq, nt, preferred_element_type=F32)
    qt_out[0] = (qt * (HEAD_DIM ** -0.5 * LOG2E)).astype(BF16)


def _ffn_qkv_layer(x, batch, fn, w_gu, w_down, kv_norm, w_k, w_vt, q_norm, w_qt):
    n = x.shape[0]
    seq = n // batch
    tiles_per_seq = seq // ROW_TILE
    t_spec = pl.BlockSpec((1, D_MODEL, ROW_TILE),
                          lambda i: (i // tiles_per_seq, 0, i % tiles_per_seq))
    t_shape = jax.ShapeDtypeStruct((batch, D_MODEL, seq), BF16)
    return pl.pallas_call(
        _ffn_qkv_kernel,
        grid=(n // ROW_TILE,),
        in_specs=[
            _row_spec(D_MODEL),
            _const_spec((1, D_MODEL)),
            _const_spec((D_MODEL, 2 * D_FF)),
            _const_spec((D_FF, D_MODEL)),
            _const_spec((1, D_MODEL)),
            _const_spec((D_MODEL, D_MODEL)),
            _const_spec((D_MODEL, D_MODEL)),
            _const_spec((1, D_MODEL)),
            _const_spec((D_MODEL, D_MODEL)),
        ],
        out_specs=[_row_spec(D_MODEL), _row_spec(D_MODEL), t_spec, t_spec,
                   pl.BlockSpec((1, N_HEADS, KN_ROWS, ROW_TILE),
                                lambda i: (i // tiles_per_seq, 0, 0, i % tiles_per_seq))],
        out_shape=[jax.ShapeDtypeStruct((n, D_MODEL), F32),
                   jax.ShapeDtypeStruct((n, D_MODEL), BF16), t_shape, t_shape,
                   jax.ShapeDtypeStruct((batch, N_HEADS, KN_ROWS, seq), F32)],
        compiler_params=_dense_params(),
        name="ffn0_qkv",
    )(x, fn, w_gu, w_down, kv_norm, w_k, w_vt, q_norm, w_qt)


def _oproj_ffn_final_kernel(x_ref, a_ref, wo_ref, fn_ref, wgu_ref, wd_ref, gn_ref, o_ref):
    x = x_ref[...] + lax.dot_general(a_ref[0], wo_ref[...], (((0,), (0,)), ((), ())),
                                     preferred_element_type=F32)
    y = _swiglu(x, fn_ref[...], wgu_ref, wd_ref)
    o_ref[...] = y * _rms_scale(y) * gn_ref[...]


def _oproj_ffn_final_layer(x, attn_t, w_o, fn, w_gu, w_down, final_norm):
    n = x.shape[0]
    tiles_per_seq = attn_t.shape[2] // ROW_TILE
    return pl.pallas_call(
        _oproj_ffn_final_kernel,
        grid=(n // ROW_TILE,),
        in_specs=[
            _row_spec(D_MODEL),
            pl.BlockSpec((1, D_MODEL, ROW_TILE),
                         lambda i: (i // tiles_per_seq, 0, i % tiles_per_seq)),
            _const_spec((D_MODEL, D_MODEL)),
            _const_spec((1, D_MODEL)),
            _const_spec((D_MODEL, 2 * D_FF)),
            _const_spec((D_FF, D_MODEL)),
            _const_spec((1, D_MODEL)),
        ],
        out_specs=_row_spec(D_MODEL),
        out_shape=jax.ShapeDtypeStruct((n, D_MODEL), F32),
        compiler_params=_dense_params(),
        name="oproj_ffn1_final",
    )(x, attn_t, w_o, fn, w_gu, w_down, final_norm)


def _attn_kernel(coef_ref, lam_ref, qt_ref, qall_ref, k_ref, vt_ref, kn_ref, sub_ref, o_ref,
                 kext_ref, qext_ref, mask_ref, kmax2_ref, lamrow_ref, s_ref, acc_ref, ctrl_ref):
    head = pl.program_id(1)
    qi = pl.program_id(2)
    streams = range(BATCH_BLOCK)
    c_parts = [coef_ref[head, i] for i in range(3)]
    c = coef_ref[head, 3]

    @pl.when(qi == 0)
    def _():
        lane = lax.broadcasted_iota(jnp.int32, (K_TILE, 128), 1)
        key_pos = lax.broadcasted_iota(jnp.int32, (K_TILE, 128), 0).astype(F32)
        kext = jnp.where(lane < 3, key_pos, 0.0)
        row = lax.broadcasted_iota(jnp.int32, (128, 2 * Q_TILE), 0)
        qry = lax.broadcasted_iota(jnp.int32, (128, 2 * Q_TILE), 1) % Q_TILE
        q_lo = -(qry % BF16_EXACT_INT).astype(F32)
        q_hi = -(qry - qry % BF16_EXACT_INT).astype(F32)
        qext = jnp.where((row >= 3) & (row < 6), q_lo, jnp.where((row >= 6) & (row < 9), q_hi, 0.0))
        for i in range(3):
            kext = jnp.where((lane == 3 + i) | (lane == 6 + i), c_parts[i], kext)
            qext = jnp.where(row == i, c_parts[i], qext)
        kext_ref[...] = kext.astype(BF16)
        qext_ref[...] = qext.astype(BF16)

        @pl.when((head == 0) & (pl.program_id(0) == 0))
        def _():
            key = lax.broadcasted_iota(jnp.int32, (K_TILE, Q_TILE), 0)
            qcol = lax.broadcasted_iota(jnp.int32, (K_TILE, Q_TILE), 1)
            for d in range(DIAG_TILES):
                hide = jnp.where(key + d * K_TILE <= qcol, 0.0, -jnp.inf)
                mask_ref[d] = jnp.concatenate([hide, hide], axis=1)

        lv = lam_ref[...]
        lam = (jnp.exp(jnp.sum(lv[0:1] * lv[1:2], axis=-1, keepdims=True))
               - jnp.exp(jnp.sum(lv[2:3] * lv[3:4], axis=-1, keepdims=True)) + LAMBDA_INIT)
        lamrow_ref[...] = jnp.broadcast_to(lam, (1, Q_TILE))

        kmax2 = []
        for st in streams:
            top_rows = jnp.max(kn_ref[st, 0], axis=1, keepdims=True)
            kmax2.append((top_rows[0:1], top_rows[1:2]))
            kmax2_ref[st] = jnp.concatenate([jnp.broadcast_to(top_rows[0:1], (1, Q_TILE)),
                                             jnp.broadcast_to(top_rows[1:2], (1, Q_TILE))], axis=1)

        def query_chunk(i, best):
            start = pl.multiple_of(i * NORM_CHUNK, NORM_CHUNK)
            for st in streams:
                q32 = qall_ref[st, :, pl.ds(start, NORM_CHUNK)].astype(F32)
                q32 = q32 * q32
                prod = jnp.concatenate(
                    [jnp.sum(q32[:HEAD_DIM], axis=0, keepdims=True) * kmax2[st][0],
                     jnp.sum(q32[HEAD_DIM:], axis=0, keepdims=True) * kmax2[st][1]], axis=1)
                best = jnp.maximum(best, prod)
            return best

        top = lax.fori_loop(0, qall_ref.shape[2] // NORM_CHUNK, query_chunk,
                            jnp.zeros((1, 2 * NORM_CHUNK), F32))
        spread = 2.04 * jnp.sqrt(jnp.max(top, axis=1, keepdims=True))
        reach = (spread + SKIP_MARGIN) / c
        pairs = jnp.floor(reach * (1.0 / (2 * K_TILE))) + 1.0
        ctrl_ref[0] = jnp.clip(pairs, 0.0, float(2 ** 20)).astype(jnp.int32)[0, 0]
        ctrl_ref[1] = jnp.where(spread < FIXED_REFERENCE_SPREAD, 1, 0)[0, 0]

    def query_matrix(st):
        qt = qt_ref[st]
        feat = lax.broadcasted_iota(jnp.int32, qt.shape, 0)
        zero = jnp.zeros_like(qt)
        return jnp.concatenate(
            [jnp.concatenate([jnp.where(feat < HEAD_DIM, qt, zero),
                              jnp.where(feat >= HEAD_DIM, qt, zero)], axis=1),
             qext_ref[...]], axis=0)

    def query_bounds(st):
        q32 = qt_ref[st].astype(F32)
        q32 = q32 * q32
        qn2 = jnp.concatenate([jnp.sum(q32[:HEAD_DIM], axis=0, keepdims=True),
                               jnp.sum(q32[HEAD_DIM:], axis=0, keepdims=True)], axis=1)
        return 1.02 * jnp.sqrt(qn2 * kmax2_ref[st])

    n_pairs = jnp.minimum(qi, ctrl_ref[0])
    fixed_reference_ok = ctrl_ref[1] == 1
    first = qi * DIAG_TILES

    def key_rows(st, kt):
        ks = pl.multiple_of(kt * K_TILE, K_TILE)
        return jnp.concatenate([k_ref[st, pl.ds(ks, K_TILE), :], kext_ref[...]], axis=1)

    def value_rows(st, kt):
        ks = pl.multiple_of(kt * K_TILE, K_TILE)
        return vt_ref[st, :, pl.ds(ks, K_TILE)]

    def tile_offset(kt):
        return c * (kt * K_TILE - qi * Q_TILE).astype(F32)

    @pl.when(fixed_reference_ok)
    def _():
        rhs = [query_matrix(st) for st in streams]
        bounds = [query_bounds(st) for st in streams]

        def tiles(t0, count, diagonal=False):
            kts, offs = [], []
            for i in range(count):
                if diagonal and i >= DIAG_TILES:
                    kts.append(jnp.maximum(t0 - i, 0))
                    offs.append(jnp.where(n_pairs > 0, tile_offset(kts[i]), -jnp.inf))
                else:
                    kts.append(t0 - i)
                    offs.append(tile_offset(kts[i]))
            los = [(DIAG_TILES - 1 - i) * K_TILE if diagonal and i < DIAG_TILES else 0
                   for i in range(count)]

            def narrow(a, lo):
                return a if lo == 0 else jnp.concatenate([a[:, lo:Q_TILE], a[:, Q_TILE + lo:]], axis=1)

            def widen(a, lo):
                if lo == 0:
                    return a
                gap = jnp.zeros((a.shape[0], lo), a.dtype)
                return jnp.concatenate([gap, a[:, :Q_TILE - lo], gap, a[:, Q_TILE - lo:]], axis=1)

            for st in streams:
                probs, row_sums = [], None
                for i in range(count):
                    s = jnp.dot(key_rows(st, kts[i]), narrow(rhs[st], los[i]),
                                preferred_element_type=F32)
                    if diagonal and i < DIAG_TILES:
                        s = s + narrow(mask_ref[DIAG_TILES - 1 - i], los[i])
                    p = jnp.exp2(s + (narrow(bounds[st], los[i]) + offs[i]))
                    folded = jnp.sum(p.reshape(K_TILE // SUM_PARTIALS, SUM_PARTIALS, p.shape[1]), axis=0)
                    folded = widen(folded, los[i])
                    row_sums = folded if row_sums is None else row_sums + folded
                    probs.append(p.astype(BF16))
                part = None
                whole = [i for i in range(count) if los[i] == 0]
                for g in range(0, len(whole), PV_TILES):
                    group = whole[g:g + PV_TILES]
                    values = jnp.concatenate([value_rows(st, kts[i]) for i in group], axis=1)
                    prod = jnp.dot(values, jnp.concatenate([probs[i] for i in group], axis=0),
                                   preferred_element_type=F32)
                    part = prod if part is None else part + prod
                for i in range(count):
                    if los[i] > 0:
                        prod = jnp.dot(value_rows(st, kts[i]), probs[i], preferred_element_type=F32)
                        part = part + widen(prod, los[i])
                if diagonal:
                    acc_ref[st, :V_DIM] = part
                    acc_ref[st, V_DIM:V_DIM + SUM_PARTIALS] = row_sums
                else:
                    acc_ref[st, :V_DIM] += part
                    acc_ref[st, V_DIM:V_DIM + SUM_PARTIALS] += row_sums

        tiles(first + DIAG_TILES - 1, DIAG_TILES + 2, diagonal=True)
        far_pairs = jnp.maximum(n_pairs - 1, 0)
        n_quads = far_pairs // 2
        far_start = first - 3

        def quad(j, carry):
            tiles(far_start - 4 * j, 4)
            return carry

        lax.fori_loop(0, n_quads, quad, 0)

        @pl.when(far_pairs % 2 == 1)
        def _():
            tiles(far_start - 4 * n_quads, 2)

    @pl.when(jnp.logical_not(fixed_reference_ok))
    def _():
        rhs = [query_matrix(st) for st in streams]
        acc_ref[...] = jnp.zeros_like(acc_ref)
        sum_rows = jnp.where(lax.broadcasted_iota(jnp.int32, (BF16_SUBLANES, K_TILE), 0) == 0,
                             1.0, 0.0).astype(BF16)

        def scores(st, kt, buf, diag=None):
            s = jnp.dot(key_rows(st, kt), rhs[st], preferred_element_type=F32)
            if diag is not None:
                s = s + mask_ref[diag]
            s_ref[st, buf] = s
            return jnp.max(s, axis=0, keepdims=True)

        def update(st, kt, buf, smax, m):
            off = tile_offset(kt)
            m_new = jnp.maximum(m, smax + off)
            alpha = jnp.exp2(m - m_new)
            p = jnp.exp2(s_ref[st, buf] - (m_new - off))
            lhs = jnp.concatenate([value_rows(st, kt), sum_rows], axis=0)
            acc_ref[st] = alpha * acc_ref[st] + jnp.dot(lhs, p.astype(BF16),
                                                        preferred_element_type=F32)
            return m_new

        m = [jnp.full((1, 2 * Q_TILE), -1e30, F32) for _ in streams]
        smax_b = [scores(st, first + 1, 1, diag=1) for st in streams]
        smax_a = [scores(st, first, 0, diag=0) for st in streams]
        m = [update(st, first + 1, 1, smax_b[st], m[st]) for st in streams]

        def pair(j, carry):
            smax_a, m = carry
            t = first - 2 * j
            smax_b = [scores(st, t - 1, 1) for st in streams]
            m = [update(st, t, 0, smax_a[st], m[st]) for st in streams]
            smax_a = [scores(st, t - 2, 0) for st in streams]
            m = [update(st, t - 1, 1, smax_b[st], m[st]) for st in streams]
            return smax_a, m

        smax_a, m = lax.fori_loop(0, n_pairs, pair, (smax_a, m))
        for st in streams:
            update(st, first - 2 * n_pairs, 0, smax_a[st], m[st])

    lam = lamrow_ref[...]
    for st in streams:
        acc = acc_ref[st, :V_DIM]
        sums = jnp.sum(acc_ref[st, V_DIM:V_DIM + SUM_PARTIALS], axis=0, keepdims=True)
        inv = 1.0 / sums
        o = (acc[:, :Q_TILE] * inv[:, :Q_TILE]
             - acc[:, Q_TILE:] * (lam * inv[:, Q_TILE:]))
        norm = lax.rsqrt(jnp.mean(o * o, axis=0, keepdims=True) + EPS) * (1.0 - LAMBDA_INIT)
        o_ref[st] = (o * norm * sub_ref[...]).astype(BF16)


def _alibi_coefficients():
    pieces = []
    rest = np.float64(LOG2E)
    for _ in range(3):
        piece = np.float64(np.asarray(rest, np.float32).astype(jnp.bfloat16).astype(np.float32))
        pieces.append(piece)
        rest = rest - piece
    rows = []
    for h in range(N_HEADS):
        slope = 2.0 ** (-8.0 * (h + 1) / N_HEADS)
        rows.append([slope * p for p in pieces] + [slope * sum(pieces)])
    return jnp.asarray(np.array(rows, dtype=np.float32))


def _diff_attention(lam_vecs, qt, k, vt, kn, subln):
    b, s, _ = k.shape
    return pl.pallas_call(
        _attn_kernel,
        grid=(b // BATCH_BLOCK, N_HEADS, s // Q_TILE),
        in_specs=[
            pl.BlockSpec(memory_space=pltpu.SMEM),
            pl.BlockSpec((4, HEAD_DIM), lambda bi, hi, qi: (0, 0)),
            pl.BlockSpec((BATCH_BLOCK, V_DIM, Q_TILE), lambda bi, hi, qi: (bi, hi, qi)),
            pl.BlockSpec((BATCH_BLOCK, V_DIM, s), lambda bi, hi, qi: (bi, hi, 0)),
            pl.BlockSpec((BATCH_BLOCK, s, V_DIM), lambda bi, hi, qi: (bi, 0, hi)),
            pl.BlockSpec((BATCH_BLOCK, V_DIM, s), lambda bi, hi, qi: (bi, hi, 0)),
            pl.BlockSpec((BATCH_BLOCK, 1, KN_ROWS, s), lambda bi, hi, qi: (bi, hi, 0, 0)),
            pl.BlockSpec((V_DIM, Q_TILE), lambda bi, hi, qi: (0, 0)),
        ],
        out_specs=pl.BlockSpec((BATCH_BLOCK, V_DIM, Q_TILE), lambda bi, hi, qi: (bi, hi, qi)),
        out_shape=jax.ShapeDtypeStruct((b, N_HEADS * V_DIM, s), BF16),
        scratch_shapes=[
            pltpu.VMEM((K_TILE, 128), BF16),
            pltpu.VMEM((128, 2 * Q_TILE), BF16),
            pltpu.VMEM((DIAG_TILES, K_TILE, 2 * Q_TILE), F32),
            pltpu.VMEM((BATCH_BLOCK, 1, 2 * Q_TILE), F32),
            pltpu.VMEM((1, Q_TILE), F32),
            pltpu.VMEM((BATCH_BLOCK, 2, K_TILE, 2 * Q_TILE), F32),
            pltpu.VMEM((BATCH_BLOCK, V_DIM + BF16_SUBLANES, 2 * Q_TILE), F32),
            pltpu.SMEM((2,), jnp.int32),
        ],
        compiler_params=pltpu.CompilerParams(
            dimension_semantics=("arbitrary", "arbitrary", "arbitrary"),
            vmem_limit_bytes=V7X_VMEM_LIMIT_BYTES),
        name="diff_attention",
    )(_alibi_coefficients(), lam_vecs, qt, qt, k, vt, kn, subln)


def kernel(x, a_norm, a_w_in, a_v_norm, a_w_sp, a_b_sp, a_w_out, ffn_norm, ffn_w_gu, ffn_w_down,
           kv_norm, kv_w, b_norm, b_w_q, b_lambda, b_subln, b_w_o, final_norm):
    b, s, d = x.shape
    assert d == D_MODEL and s % Q_TILE == 0 and s % ROW_TILE == 0
    assert DIAG_TILES == 2 and K_TILE <= BF16_EXACT_INT and b % BATCH_BLOCK == 0
    xs = x.reshape(b * s, d)
    row = lambda g: g.reshape(1, -1).astype(F32)

    b_full = jnp.repeat(a_b_sp[0].T.astype(F32), GROUP_DIM, axis=1)
    xs = _gmlp_layer(xs, row(a_norm[0]), a_w_in[0].astype(BF16), row(a_v_norm[0]),
                     a_w_sp[0].astype(F32), b_full, a_w_out[0].astype(BF16))

    w_k = kv_w[:, :D_MODEL].astype(BF16)
    w_vt = kv_w[:, D_MODEL:].T.astype(BF16)
    w_qt = b_w_q[0].T.astype(BF16)
    xs, k, qt, vt, kn = _ffn_qkv_layer(xs, b, row(ffn_norm[0]), ffn_w_gu[0].astype(BF16),
                                       ffn_w_down[0].astype(BF16), row(kv_norm), w_k, w_vt,
                                       row(b_norm[0]), w_qt)

    sub_cols = jnp.broadcast_to(b_subln[0].astype(F32)[:, None], (V_DIM, Q_TILE))
    attn = _diff_attention(b_lambda[0].astype(F32), qt, k.reshape(b, s, d), vt, kn, sub_cols)

    out = _oproj_ffn_final_layer(xs, attn, b_w_o[0].astype(BF16),
                                 row(ffn_norm[1]), ffn_w_gu[1].astype(BF16),
                                 ffn_w_down[1].astype(BF16), row(final_norm))
    return out.reshape(b, s, d)
```

```python
import math

import jax
import jax.numpy as jnp
import numpy as np
from jax import lax
from jax.experimental import pallas as pl
from jax.experimental.pallas import tpu as pltpu

D_MODEL = 1024
CHUNK = 128
N_GROUPS = 8
GROUP_DIM = D_MODEL // N_GROUPS
N_HEADS = 8
HEAD_DIM = 64
V_DIM = 2 * HEAD_DIM
D_FF = 2816
EPS = 1e-6
N_A = 1
LAMBDA_INIT = 0.8 - 0.6 * math.exp(-0.3 * N_A)
LOG2E = math.log2(math.e)

V7X_VMEM_LIMIT_BYTES = 56 * 1024 * 1024
BF16_EXACT_INT = 256
BF16_SUBLANES = 16

ROW_TILE = 512
GMLP_ROW_TILE = 1024
Q_TILE = 512
K_TILE = 256
DIAG_TILES = Q_TILE // K_TILE
SUM_PARTIALS = 8
NORM_CHUNK = 1024
KN_ROWS = 8
PV_TILES = 4
BATCH_BLOCK = 2
SKIP_MARGIN = 160.0
FIXED_REFERENCE_SPREAD = 100.0

F32 = jnp.float32
BF16 = jnp.bfloat16


def _rms_scale(x):
    return lax.rsqrt(jnp.mean(x * x, axis=-1, keepdims=True) + EPS)


def _gelu_tanh(x):
    c = math.sqrt(2.0 / math.pi)
    return 0.5 * x * (1.0 + jnp.tanh(c * (x + 0.044715 * (x * x * x))))


def _silu(x):
    return 0.5 * x * (1.0 + jnp.tanh(0.5 * x))


def _const_spec(shape):
    nd = len(shape)
    return pl.BlockSpec(shape, lambda *_: (0,) * nd, pipeline_mode=pl.Buffered(1))


def _row_spec(width):
    return pl.BlockSpec((ROW_TILE, width), lambda i: (i, 0))


def _dense_params():
    return pltpu.CompilerParams(dimension_semantics=("arbitrary",),
                                vmem_limit_bytes=V7X_VMEM_LIMIT_BYTES)


def _gmlp_kernel(x_ref, an_ref, win_ref, vg_ref, wsp_ref, bsp_ref, wout_ref, o_ref):
    t_idx = lax.broadcasted_iota(jnp.int32, (CHUNK, CHUNK), 0)
    s_idx = lax.broadcasted_iota(jnp.int32, (CHUNK, CHUNK), 1)
    causal = s_idx <= t_idx
    n_chunks = GMLP_ROW_TILE // CHUNK

    x = x_ref[...]
    h = (x * _rms_scale(x) * an_ref[...]).astype(BF16)
    uv = _gelu_tanh(jnp.dot(h, win_ref[...], preferred_element_type=F32))
    u = uv[:, :D_MODEL]
    v = uv[:, D_MODEL:]
    mu = jnp.mean(v, axis=-1, keepdims=True)
    vc = v - mu
    var = jnp.mean(vc * vc, axis=-1, keepdims=True)
    vn = (vc * lax.rsqrt(var + EPS) * vg_ref[...]).astype(BF16)
    cols = []
    for g in range(N_GROUPS):
        wm = jnp.where(causal, wsp_ref[g], 0.0).astype(BF16)
        bias = bsp_ref[:, g * GROUP_DIM:(g + 1) * GROUP_DIM]
        vg = jnp.concatenate([vn[c * CHUNK:(c + 1) * CHUNK, g * GROUP_DIM:(g + 1) * GROUP_DIM]
                              for c in range(n_chunks)], axis=1)
        zg = jnp.dot(wm, vg, preferred_element_type=F32)
        cols.append(jnp.concatenate([zg[:, c * GROUP_DIM:(c + 1) * GROUP_DIM] + bias
                                     for c in range(n_chunks)], axis=0))
    z = jnp.concatenate(cols, axis=1)
    gated = (u * z).astype(BF16)
    o_ref[...] = x + jnp.dot(gated, wout_ref[...], preferred_element_type=F32)


def _gmlp_layer(x, a_norm, w_in, v_gain, w_sp, b_full, w_out):
    n = x.shape[0]
    return pl.pallas_call(
        _gmlp_kernel,
        grid=(n // GMLP_ROW_TILE,),
        in_specs=[
            pl.BlockSpec((GMLP_ROW_TILE, D_MODEL), lambda i: (i, 0)),
            _const_spec((1, D_MODEL)),
            _const_spec((D_MODEL, 2 * D_MODEL)),
            _const_spec((1, D_MODEL)),
            _const_spec((N_GROUPS, CHUNK, CHUNK)),
            _const_spec((CHUNK, D_MODEL)),
            _const_spec((D_MODEL, D_MODEL)),
        ],
        out_specs=pl.BlockSpec((GMLP_ROW_TILE, D_MODEL), lambda i: (i, 0)),
        out_shape=jax.ShapeDtypeStruct((n, D_MODEL), F32),
        compiler_params=_dense_params(),
        name="gmlp_mixer",
    )(x, a_norm, w_in, v_gain, w_sp, b_full, w_out)


def _swiglu(x, fn, wgu_ref, wd_ref):
    h = (x * _rms_scale(x) * fn).astype(BF16)
    gu = jnp.dot(h, wgu_ref[...], preferred_element_type=F32)
    a = (_silu(gu[:, :D_FF]) * gu[:, D_FF:]).astype(BF16)
    return x + jnp.dot(a, wd_ref[...], preferred_element_type=F32)


def _ffn_qkv_kernel(x_ref, fn_ref, wgu_ref, wd_ref, kvn_ref, wk_ref, wvt_ref, qn_ref, wqt_ref,
                    x_out, k_out, qt_out, vt_out, kn_out):
    y = _swiglu(x_ref[...], fn_ref[...], wgu_ref, wd_ref)
    x_out[...] = y
    yn = y * _rms_scale(y)
    hk = (yn * kvn_ref[...]).astype(BF16)
    hq = (yn * qn_ref[...]).astype(BF16)
    kb = jnp.dot(hk, wk_ref[...], preferred_element_type=F32).astype(BF16)
    k_out[...] = kb
    nt = (((1,), (1,)), ((), ()))
    row = lax.broadcasted_iota(jnp.int32, (N_HEADS * KN_ROWS, D_MODEL), 0)
    col = lax.broadcasted_iota(jnp.int32, (N_HEADS * KN_ROWS, D_MODEL), 1)
    half = row % KN_ROWS
    pick = (half < 2) & (col // HEAD_DIM == 2 * (row // KN_ROWS) + half)
    sq = lax.dot_general(jnp.where(pick, 1.0, 0.0).astype(BF16), kb * kb, nt,
                         preferred_element_type=F32)
    kn_out[0] = sq.reshape(N_HEADS, KN_ROWS, sq.shape[1])
    vt_out[0] = lax.dot_general(wvt_ref[...], hk, nt, preferred_element_type=F32).astype(BF16)
    qt = lax.dot_general(wqt_ref[...], hq, nt, preferred_element_type=F32)
    qt_out[0] = (qt * (HEAD_DIM ** -0.5 * LOG2E)).astype(BF16)


def _ffn_qkv_layer(x, batch, fn, w_gu, w_down, kv_norm, w_k, w_vt, q_norm, w_qt):
    n = x.shape[0]
    seq = n // batch
    tiles_per_seq = seq // ROW_TILE
    t_spec = pl.BlockSpec((1, D_MODEL, ROW_TILE),
                          lambda i: (i // tiles_per_seq, 0, i % tiles_per_seq))
    t_shape = jax.ShapeDtypeStruct((batch, D_MODEL, seq), BF16)
    return pl.pallas_call(
        _ffn_qkv_kernel,
        grid=(n // ROW_TILE,),
        in_specs=[
            _row_spec(D_MODEL),
            _const_spec((1, D_MODEL)),
            _const_spec((D_MODEL, 2 * D_FF)),
            _const_spec((D_FF, D_MODEL)),
            _const_spec((1, D_MODEL)),
            _const_spec((D_MODEL, D_MODEL)),
            _const_spec((D_MODEL, D_MODEL)),
            _const_spec((1, D_MODEL)),
            _const_spec((D_MODEL, D_MODEL)),
        ],
        out_specs=[_row_spec(D_MODEL), _row_spec(D_MODEL), t_spec, t_spec,
                   pl.BlockSpec((1, N_HEADS, KN_ROWS, ROW_TILE),
                                lambda i: (i // tiles_per_seq, 0, 0, i % tiles_per_seq))],
        out_shape=[jax.ShapeDtypeStruct((n, D_MODEL), F32),
                   jax.ShapeDtypeStruct((n, D_MODEL), BF16), t_shape, t_shape,
                   jax.ShapeDtypeStruct((batch, N_HEADS, KN_ROWS, seq), F32)],
        compiler_params=_dense_params(),
        name="ffn0_qkv",
    )(x, fn, w_gu, w_down, kv_norm, w_k, w_vt, q_norm, w_qt)


def _oproj_ffn_final_kernel(x_ref, a_ref, wo_ref, fn_ref, wgu_ref, wd_ref, gn_ref, o_ref):
    x = x_ref[...] + lax.dot_general(a_ref[0], wo_ref[...], (((0,), (0,)), ((), ())),
                                     preferred_element_type=F32)
    y = _swiglu(x, fn_ref[...], wgu_ref, wd_ref)
    o_ref[...] = y * _rms_scale(y) * gn_ref[...]


def _oproj_ffn_final_layer(x, attn_t, w_o, fn, w_gu, w_down, final_norm):
    n = x.shape[0]
    tiles_per_seq = attn_t.shape[2] // ROW_TILE
    return pl.pallas_call(
        _oproj_ffn_final_kernel,
        grid=(n // ROW_TILE,),
        in_specs=[
            _row_spec(D_MODEL),
            pl.BlockSpec((1, D_MODEL, ROW_TILE),
                         lambda i: (i // tiles_per_seq, 0, i % tiles_per_seq)),
            _const_spec((D_MODEL, D_MODEL)),
            _const_spec((1, D_MODEL)),
            _const_spec((D_MODEL, 2 * D_FF)),
            _const_spec((D_FF, D_MODEL)),
            _const_spec((1, D_MODEL)),
        ],
        out_specs=_row_spec(D_MODEL),
        out_shape=jax.ShapeDtypeStruct((n, D_MODEL), F32),
        compiler_params=_dense_params(),
        name="oproj_ffn1_final",
    )(x, attn_t, w_o, fn, w_gu, w_down, final_norm)


def _attn_kernel(coef_ref, lam_ref, qt_ref, qall_ref, k_ref, vt_ref, kn_ref, sub_ref, o_ref,
                 kext_ref, qext_ref, mask_ref, kmax2_ref, lamrow_ref, s_ref, acc_ref, ctrl_ref):
    head = pl.program_id(1)
    qi = pl.program_id(2)
    streams = range(BATCH_BLOCK)
    c_parts = [coef_ref[head, i] for i in range(3)]
    c = coef_ref[head, 3]

    @pl.when(qi == 0)
    def _():
        lane = lax.broadcasted_iota(jnp.int32, (K_TILE, 128), 1)
        key_pos = lax.broadcasted_iota(jnp.int32, (K_TILE, 128), 0).astype(F32)
        kext = jnp.where(lane < 3, key_pos, 0.0)
        row = lax.broadcasted_iota(jnp.int32, (128, 2 * Q_TILE), 0)
        qry = lax.broadcasted_iota(jnp.int32, (128, 2 * Q_TILE), 1) % Q_TILE
        q_lo = -(qry % BF16_EXACT_INT).astype(F32)
        q_hi = -(qry - qry % BF16_EXACT_INT).astype(F32)
        qext = jnp.where((row >= 3) & (row < 6), q_lo, jnp.where((row >= 6) & (row < 9), q_hi, 0.0))
        for i in range(3):
            kext = jnp.where((lane == 3 + i) | (lane == 6 + i), c_parts[i], kext)
            qext = jnp.where(row == i, c_parts[i], qext)
        kext_ref[...] = kext.astype(BF16)
        qext_ref[...] = qext.astype(BF16)

        @pl.when((head == 0) & (pl.program_id(0) == 0))
        def _():
            key = lax.broadcasted_iota(jnp.int32, (K_TILE, Q_TILE), 0)
            qcol = lax.broadcasted_iota(jnp.int32, (K_TILE, Q_TILE), 1)
            for d in range(DIAG_TILES):
                hide = jnp.where(key + d * K_TILE <= qcol, 0.0, -jnp.inf)
                mask_ref[d] = jnp.concatenate([hide, hide], axis=1)

        lv = lam_ref[...]
        lam = (jnp.exp(jnp.sum(lv[0:1] * lv[1:2], axis=-1, keepdims=True))
               - jnp.exp(jnp.sum(lv[2:3] * lv[3:4], axis=-1, keepdims=True)) + LAMBDA_INIT)
        lamrow_ref[...] = jnp.broadcast_to(lam, (1, Q_TILE))

        kmax2 = []
        for st in streams:
            top_rows = jnp.max(kn_ref[st, 0], axis=1, keepdims=True)
            kmax2.append((top_rows[0:1], top_rows[1:2]))
            kmax2_ref[st] = jnp.concatenate([jnp.broadcast_to(top_rows[0:1], (1, Q_TILE)),
                                             jnp.broadcast_to(top_rows[1:2], (1, Q_TILE))], axis=1)

        def query_chunk(i, best):
            start = pl.multiple_of(i * NORM_CHUNK, NORM_CHUNK)
            for st in streams:
                q32 = qall_ref[st, :, pl.ds(start, NORM_CHUNK)].astype(F32)
                q32 = q32 * q32
                prod = jnp.concatenate(
                    [jnp.sum(q32[:HEAD_DIM], axis=0, keepdims=True) * kmax2[st][0],
                     jnp.sum(q32[HEAD_DIM:], axis=0, keepdims=True) * kmax2[st][1]], axis=1)
                best = jnp.maximum(best, prod)
            return best

        top = lax.fori_loop(0, qall_ref.shape[2] // NORM_CHUNK, query_chunk,
                            jnp.zeros((1, 2 * NORM_CHUNK), F32))
        spread = 2.04 * jnp.sqrt(jnp.max(top, axis=1, keepdims=True))
        reach = (spread + SKIP_MARGIN) / c
        pairs = jnp.floor(reach * (1.0 / (2 * K_TILE))) + 1.0
        ctrl_ref[0] = jnp.clip(pairs, 0.0, float(2 ** 20)).astype(jnp.int32)[0, 0]
        ctrl_ref[1] = jnp.where(spread < FIXED_REFERENCE_SPREAD, 1, 0)[0, 0]

    def query_matrix(st):
        qt = qt_ref[st]
        feat = lax.broadcasted_iota(jnp.int32, qt.shape, 0)
        zero = jnp.zeros_like(qt)
        return jnp.concatenate(
            [jnp.concatenate([jnp.where(feat < HEAD_DIM, qt, zero),
                              jnp.where(feat >= HEAD_DIM, qt, zero)], axis=1),
             qext_ref[...]], axis=0)

    def query_bounds(st):
        q32 = qt_ref[st].astype(F32)
        q32 = q32 * q32
        qn2 = jnp.concatenate([jnp.sum(q32[:HEAD_DIM], axis=0, keepdims=True),
                               jnp.sum(q32[HEAD_DIM:], axis=0, keepdims=True)], axis=1)
        return 1.02 * jnp.sqrt(qn2 * kmax2_ref[st])

    n_pairs = jnp.minimum(qi, ctrl_ref[0])
    fixed_reference_ok = ctrl_ref[1] == 1
    first = qi * DIAG_TILES

    def key_rows(st, kt):
        ks = pl.multiple_of(kt * K_TILE, K_TILE)
        return jnp.concatenate([k_ref[st, pl.ds(ks, K_TILE), :], kext_ref[...]], axis=1)

    def value_rows(st, kt):
        ks = pl.multiple_of(kt * K_TILE, K_TILE)
        return vt_ref[st, :, pl.ds(ks, K_TILE)]

    def tile_offset(kt):
        return c * (kt * K_TILE - qi * Q_TILE).astype(F32)

    @pl.when(fixed_reference_ok)
    def _():
        rhs = [query_matrix(st) for st in streams]
        bounds = [query_bounds(st) for st in streams]

        def tiles(t0, count, diagonal=False):
            kts, offs = [], []
            for i in range(count):
                if diagonal and i >= DIAG_TILES:
                    kts.append(jnp.maximum(t0 - i, 0))
                    offs.append(jnp.where(n_pairs > 0, tile_offset(kts[i]), -jnp.inf))
                else:
                    kts.append(t0 - i)
                    offs.append(tile_offset(kts[i]))
            los = [(DIAG_TILES - 1 - i) * K_TILE if diagonal and i < DIAG_TILES else 0
                   for i in range(count)]

            def narrow(a, lo):
                return a if lo == 0 else jnp.concatenate([a[:, lo:Q_TILE], a[:, Q_TILE + lo:]], axis=1)

            def widen(a, lo):
                if lo == 0:
                    return a
                gap = jnp.zeros((a.shape[0], lo), a.dtype)
                return jnp.concatenate([gap, a[:, :Q_TILE - lo], gap, a[:, Q_TILE - lo:]], axis=1)

            for st in streams:
                probs, row_sums = [], None
                for i in range(count):
                    s = jnp.dot(key_rows(st, kts[i]), narrow(rhs[st], los[i]),
                                preferred_element_type=F32)
                    if diagonal and i < DIAG_TILES:
                        s = s + narrow(mask_ref[DIAG_TILES - 1 - i], los[i])
                    p = jnp.exp2(s + (narrow(bounds[st], los[i]) + offs[i]))
                    folded = jnp.sum(p.reshape(K_TILE // SUM_PARTIALS, SUM_PARTIALS, p.shape[1]), axis=0)
                    folded = widen(folded, los[i])
                    row_sums = folded if row_sums is None else row_sums + folded
                    probs.append(p.astype(BF16))
                part = None
                whole = [i for i in range(count) if los[i] == 0]
                for g in range(0, len(whole), PV_TILES):
                    group = whole[g:g + PV_TILES]
                    values = jnp.concatenate([value_rows(st, kts[i]) for i in group], axis=1)
                    prod = jnp.dot(values, jnp.concatenate([probs[i] for i in group], axis=0),
                                   preferred_element_type=F32)
                    part = prod if part is None else part + prod
                for i in range(count):
                    if los[i] > 0:
                        prod = jnp.dot(value_rows(st, kts[i]), probs[i], preferred_element_type=F32)
                        part = part + widen(prod, los[i])
                if diagonal:
                    acc_ref[st, :V_DIM] = part
                    acc_ref[st, V_DIM:V_DIM + SUM_PARTIALS] = row_sums
                else:
                    acc_ref[st, :V_DIM] += part
                    acc_ref[st, V_DIM:V_DIM + SUM_PARTIALS] += row_sums

        tiles(first + DIAG_TILES - 1, DIAG_TILES + 2, diagonal=True)
        far_pairs = jnp.maximum(n_pairs - 1, 0)
        n_quads = far_pairs // 2
        far_start = first - 3

        def quad(j, carry):
            tiles(far_start - 4 * j, 4)
            return carry

        lax.fori_loop(0, n_quads, quad, 0)

        @pl.when(far_pairs % 2 == 1)
        def _():
            tiles(far_start - 4 * n_quads, 2)

    @pl.when(jnp.logical_not(fixed_reference_ok))
    def _():
        rhs = [query_matrix(st) for st in streams]
        acc_ref[...] = jnp.zeros_like(acc_ref)
        sum_rows = jnp.where(lax.broadcasted_iota(jnp.int32, (BF16_SUBLANES, K_TILE), 0) == 0,
                             1.0, 0.0).astype(BF16)

        def scores(st, kt, buf, diag=None):
            s = jnp.dot(key_rows(st, kt), rhs[st], preferred_element_type=F32)
            if diag is not None:
                s = s + mask_ref[diag]
            s_ref[st, buf] = s
            return jnp.max(s, axis=0, keepdims=True)

        def update(st, kt, buf, smax, m):
            off = tile_offset(kt)
            m_new = jnp.maximum(m, smax + off)
            alpha = jnp.exp2(m - m_new)
            p = jnp.exp2(s_ref[st, buf] - (m_new - off))
            lhs = jnp.concatenate([value_rows(st, kt), sum_rows], axis=0)
            acc_ref[st] = alpha * acc_ref[st] + jnp.dot(lhs, p.astype(BF16),
                                                        preferred_element_type=F32)
            return m_new

        m = [jnp.full((1, 2 * Q_TILE), -1e30, F32) for _ in streams]
        smax_b = [scores(st, first + 1, 1, diag=1) for st in streams]
        smax_a = [scores(st, first, 0, diag=0) for st in streams]
        m = [update(st, first + 1, 1, smax_b[st], m[st]) for st in streams]

        def pair(j, carry):
            smax_a, m = carry
            t = first - 2 * j
            smax_b = [scores(st, t - 1, 1) for st in streams]
            m = [update(st, t, 0, smax_a[st], m[st]) for st in streams]
            smax_a = [scores(st, t - 2, 0) for st in streams]
            m = [update(st, t - 1, 1, smax_b[st], m[st]) for st in streams]
            return smax_a, m

        smax_a, m = lax.fori_loop(0, n_pairs, pair, (smax_a, m))
        for st in streams:
            update(st, first - 2 * n_pairs, 0, smax_a[st], m[st])

    lam = lamrow_ref[...]
    for st in streams:
        acc = acc_ref[st, :V_DIM]
        sums = jnp.sum(acc_ref[st, V_DIM:V_DIM + SUM_PARTIALS], axis=0, keepdims=True)
        inv = 1.0 / sums
        o = (acc[:, :Q_TILE] * inv[:, :Q_TILE]
             - acc[:, Q_TILE:] * (lam * inv[:, Q_TILE:]))
        norm = lax.rsqrt(jnp.mean(o * o, axis=0, keepdims=True) + EPS) * (1.0 - LAMBDA_INIT)
        o_ref[st] = (o * norm * sub_ref[...]).astype(BF16)


def _alibi_coefficients():
    pieces = []
    rest = np.float64(LOG2E)
    for _ in range(3):
        piece = np.float64(np.asarray(rest, np.float32).astype(jnp.bfloat16).astype(np.float32))
        pieces.append(piece)
        rest = rest - piece
    rows = []
    for h in range(N_HEADS):
        slope = 2.0 ** (-8.0 * (h + 1) / N_HEADS)
        rows.append([slope * p for p in pieces] + [slope * sum(pieces)])
    return jnp.asarray(np.array(rows, dtype=np.float32))


def _diff_attention(lam_vecs, qt, k, vt, kn, subln):
    b, s, _ = k.shape
    return pl.pallas_call(
        _attn_kernel,
        grid=(b // BATCH_BLOCK, N_HEADS, s // Q_TILE),
        in_specs=[
            pl.BlockSpec(memory_space=pltpu.SMEM),
            pl.BlockSpec((4, HEAD_DIM), lambda bi, hi, qi: (0, 0)),
            pl.BlockSpec((BATCH_BLOCK, V_DIM, Q_TILE), lambda bi, hi, qi: (bi, hi, qi)),
            pl.BlockSpec((BATCH_BLOCK, V_DIM, s), lambda bi, hi, qi: (bi, hi, 0)),
            pl.BlockSpec((BATCH_BLOCK, s, V_DIM), lambda bi, hi, qi: (bi, 0, hi)),
            pl.BlockSpec((BATCH_BLOCK, V_DIM, s), lambda bi, hi, qi: (bi, hi, 0)),
            pl.BlockSpec((BATCH_BLOCK, 1, KN_ROWS, s), lambda bi, hi, qi: (bi, hi, 0, 0)),
            pl.BlockSpec((V_DIM, Q_TILE), lambda bi, hi, qi: (0, 0)),
        ],
        out_specs=pl.BlockSpec((BATCH_BLOCK, V_DIM, Q_TILE), lambda bi, hi, qi: (bi, hi, qi)),
        out_shape=jax.ShapeDtypeStruct((b, N_HEADS * V_DIM, s), BF16),
        scratch_shapes=[
            pltpu.VMEM((K_TILE, 128), BF16),
            pltpu.VMEM((128, 2 * Q_TILE), BF16),
            pltpu.VMEM((DIAG_TILES, K_TILE, 2 * Q_TILE), F32),
            pltpu.VMEM((BATCH_BLOCK, 1, 2 * Q_TILE), F32),
            pltpu.VMEM((1, Q_TILE), F32),
            pltpu.VMEM((BATCH_BLOCK, 2, K_TILE, 2 * Q_TILE), F32),
            pltpu.VMEM((BATCH_BLOCK, V_DIM + BF16_SUBLANES, 2 * Q_TILE), F32),
            pltpu.SMEM((2,), jnp.int32),
        ],
        compiler_params=pltpu.CompilerParams(
            dimension_semantics=("arbitrary", "arbitrary", "arbitrary"),
            vmem_limit_bytes=V7X_VMEM_LIMIT_BYTES),
        name="diff_attention",
    )(_alibi_coefficients(), lam_vecs, qt, qt, k, vt, kn, subln)


def kernel(x, a_norm, a_w_in, a_v_norm, a_w_sp, a_b_sp, a_w_out, ffn_norm, ffn_w_gu, ffn_w_down,
           kv_norm, kv_w, b_norm, b_w_q, b_lambda, b_subln, b_w_o, final_norm):
    b, s, d = x.shape
    assert d == D_MODEL and s % Q_TILE == 0 and s % ROW_TILE == 0
    assert DIAG_TILES == 2 and K_TILE <= BF16_EXACT_INT and b % BATCH_BLOCK == 0
    xs = x.reshape(b * s, d)
    row = lambda g: g.reshape(1, -1).astype(F32)

    b_full = jnp.repeat(a_b_sp[0].T.astype(F32), GROUP_DIM, axis=1)
    xs = _gmlp_layer(xs, row(a_norm[0]), a_w_in[0].astype(BF16), row(a_v_norm[0]),
                     a_w_sp[0].astype(F32), b_full, a_w_out[0].astype(BF16))

    w_k = kv_w[:, :D_MODEL].astype(BF16)
    w_vt = kv_w[:, D_MODEL:].T.astype(BF16)
    w_qt = b_w_q[0].T.astype(BF16)
    xs, k, qt, vt, kn = _ffn_qkv_layer(xs, b, row(ffn_norm[0]), ffn_w_gu[0].astype(BF16),
                                       ffn_w_down[0].astype(BF16), row(kv_norm), w_k, w_vt,
                                       row(b_norm[0]), w_qt)

    sub_cols = jnp.broadcast_to(b_subln[0].astype(F32)[:, None], (V_DIM, Q_TILE))
    attn = _diff_attention(b_lambda[0].astype(F32), qt, k.reshape(b, s, d), vt, kn, sub_cols)

    out = _oproj_ffn_final_layer(xs, attn, b_w_o[0].astype(BF16),
                                 row(ffn_norm[1]), ffn_w_gu[1].astype(BF16),
                                 ffn_w_down[1].astype(BF16), row(final_norm))
    return out.reshape(b, s, d)
```

```python
import math

import jax
import jax.numpy as jnp
import numpy as np
from jax import lax
from jax.experimental import pallas as pl
from jax.experimental.pallas import tpu as pltpu

D_MODEL = 1024
CHUNK = 128
N_GROUPS = 8
GROUP_DIM = D_MODEL // N_GROUPS
N_HEADS = 8
HEAD_DIM = 64
V_DIM = 2 * HEAD_DIM
D_FF = 2816
EPS = 1e-6
N_A = 1
LAMBDA_INIT = 0.8 - 0.6 * math.exp(-0.3 * N_A)
LOG2E = math.log2(math.e)

V7X_VMEM_LIMIT_BYTES = 56 * 1024 * 1024
BF16_EXACT_INT = 256
BF16_SUBLANES = 16

ROW_TILE = 512
GMLP_ROW_TILE = 1024
Q_TILE = 512
K_TILE = 256
DIAG_TILES = Q_TILE // K_TILE
SUM_PARTIALS = 8
NORM_CHUNK = 1024
KN_ROWS = 8
PV_TILES = 4
PV_TILES_FAR = 2
BATCH_BLOCK = 2
SKIP_MARGIN = 160.0
FIXED_REFERENCE_SPREAD = 100.0

F32 = jnp.float32
BF16 = jnp.bfloat16


def _rms_scale(x):
    return lax.rsqrt(jnp.mean(x * x, axis=-1, keepdims=True) + EPS)


def _gelu_tanh(x):
    c = math.sqrt(2.0 / math.pi)
    return 0.5 * x * (1.0 + jnp.tanh(c * (x + 0.044715 * (x * x * x))))


def _silu(x):
    return 0.5 * x * (1.0 + jnp.tanh(0.5 * x))


def _const_spec(shape):
    nd = len(shape)
    return pl.BlockSpec(shape, lambda *_: (0,) * nd, pipeline_mode=pl.Buffered(1))


def _row_spec(width):
    return pl.BlockSpec((ROW_TILE, width), lambda i: (i, 0))


def _dense_params():
    return pltpu.CompilerParams(dimension_semantics=("arbitrary",),
                                vmem_limit_bytes=V7X_VMEM_LIMIT_BYTES)


def _gmlp_kernel(x_ref, an_ref, win_ref, vg_ref, wsp_ref, bsp_ref, wout_ref, o_ref):
    t_idx = lax.broadcasted_iota(jnp.int32, (CHUNK, CHUNK), 0)
    s_idx = lax.broadcasted_iota(jnp.int32, (CHUNK, CHUNK), 1)
    causal = s_idx <= t_idx
    n_chunks = GMLP_ROW_TILE // CHUNK

    x = x_ref[...]
    h = (x * _rms_scale(x) * an_ref[...]).astype(BF16)
    uv = _gelu_tanh(jnp.dot(h, win_ref[...], preferred_element_type=F32))
    u = uv[:, :D_MODEL]
    v = uv[:, D_MODEL:]
    mu = jnp.mean(v, axis=-1, keepdims=True)
    vc = v - mu
    var = jnp.mean(vc * vc, axis=-1, keepdims=True)
    vn = (vc * lax.rsqrt(var + EPS) * vg_ref[...]).astype(BF16)
    cols = []
    for g in range(N_GROUPS):
        wm = jnp.where(causal, wsp_ref[g], 0.0).astype(BF16)
        bias = bsp_ref[:, g * GROUP_DIM:(g + 1) * GROUP_DIM]
        vg = jnp.concatenate([vn[c * CHUNK:(c + 1) * CHUNK, g * GROUP_DIM:(g + 1) * GROUP_DIM]
                              for c in range(n_chunks)], axis=1)
        zg = jnp.dot(wm, vg, preferred_element_type=F32)
        cols.append(jnp.concatenate([zg[:, c * GROUP_DIM:(c + 1) * GROUP_DIM] + bias
                                     for c in range(n_chunks)], axis=0))
    z = jnp.concatenate(cols, axis=1)
    gated = (u * z).astype(BF16)
    o_ref[...] = x + jnp.dot(gated, wout_ref[...], preferred_element_type=F32)


def _gmlp_layer(x, a_norm, w_in, v_gain, w_sp, b_full, w_out):
    n = x.shape[0]
    return pl.pallas_call(
        _gmlp_kernel,
        grid=(n // GMLP_ROW_TILE,),
        in_specs=[
            pl.BlockSpec((GMLP_ROW_TILE, D_MODEL), lambda i: (i, 0)),
            _const_spec((1, D_MODEL)),
            _const_spec((D_MODEL, 2 * D_MODEL)),
            _const_spec((1, D_MODEL)),
            _const_spec((N_GROUPS, CHUNK, CHUNK)),
            _const_spec((CHUNK, D_MODEL)),
            _const_spec((D_MODEL, D_MODEL)),
        ],
        out_specs=pl.BlockSpec((GMLP_ROW_TILE, D_MODEL), lambda i: (i, 0)),
        out_shape=jax.ShapeDtypeStruct((n, D_MODEL), F32),
        compiler_params=_dense_params(),
        name="gmlp_mixer",
    )(x, a_norm, w_in, v_gain, w_sp, b_full, w_out)


def _swiglu(x, fn, wgu_ref, wd_ref):
    h = (x * _rms_scale(x) * fn).astype(BF16)
    gu = jnp.dot(h, wgu_ref[...], preferred_element_type=F32)
    a = (_silu(gu[:, :D_FF]) * gu[:, D_FF:]).astype(BF16)
    return x + jnp.dot(a, wd_ref[...], preferred_element_type=F32)


def _ffn_qkv_kernel(x_ref, fn_ref, wgu_ref, wd_ref, kvn_ref, wk_ref, wvt_ref, qn_ref, wqt_ref,
                    x_out, k_out, qt_out, vt_out, kn_out):
    y = _swiglu(x_ref[...], fn_ref[...], wgu_ref, wd_ref)
    x_out[...] = y
    yn = y * _rms_scale(y)
    hk = (yn * kvn_ref[...]).astype(BF16)
    hq = (yn * qn_ref[...]).astype(BF16)
    kb = jnp.dot(hk, wk_ref[...], preferred_element_type=F32).astype(BF16)
    k_out[...] = kb
    nt = (((1,), (1,)), ((), ()))
    row = lax.broadcasted_iota(jnp.int32, (N_HEADS * KN_ROWS, D_MODEL), 0)
    col = lax.broadcasted_iota(jnp.int32, (N_HEADS * KN_ROWS, D_MODEL), 1)
    half = row % KN_ROWS
    pick = (half < 2) & (col // HEAD_DIM == 2 * (row // KN_ROWS) + half)
    sq = lax.dot_general(jnp.where(pick, 1.0, 0.0).astype(BF16), kb * kb, nt,
                         preferred_element_type=F32)
    kn_out[0] = sq.reshape(N_HEADS, KN_ROWS, sq.shape[1])
    vt_out[0] = lax.dot_general(wvt_ref[...], hk, nt, preferred_element_type=F32).astype(BF16)
    qt = lax.dot_general(wqt_ref[...], hq, nt, preferred_element_type=F32)
    qt_out[0] = (qt * (HEAD_DIM ** -0.5 * LOG2E)).astype(BF16)


def _ffn_qkv_layer(x, batch, fn, w_gu, w_down, kv_norm, w_k, w_vt, q_norm, w_qt):
    n = x.shape[0]
    seq = n // batch
    tiles_per_seq = seq // ROW_TILE
    t_spec = pl.BlockSpec((1, D_MODEL, ROW_TILE),
                          lambda i: (i // tiles_per_seq, 0, i % tiles_per_seq))
    t_shape = jax.ShapeDtypeStruct((batch, D_MODEL, seq), BF16)
    return pl.pallas_call(
        _ffn_qkv_kernel,
        grid=(n // ROW_TILE,),
        in_specs=[
            _row_spec(D_MODEL),
            _const_spec((1, D_MODEL)),
            _const_spec((D_MODEL, 2 * D_FF)),
            _const_spec((D_FF, D_MODEL)),
            _const_spec((1, D_MODEL)),
            _const_spec((D_MODEL, D_MODEL)),
            _const_spec((D_MODEL, D_MODEL)),
            _const_spec((1, D_MODEL)),
            _const_spec((D_MODEL, D_MODEL)),
        ],
        out_specs=[_row_spec(D_MODEL), _row_spec(D_MODEL), t_spec, t_spec,
                   pl.BlockSpec((1, N_HEADS, KN_ROWS, ROW_TILE),
                                lambda i: (i // tiles_per_seq, 0, 0, i % tiles_per_seq))],
        out_shape=[jax.ShapeDtypeStruct((n, D_MODEL), F32),
                   jax.ShapeDtypeStruct((n, D_MODEL), BF16), t_shape, t_shape,
                   jax.ShapeDtypeStruct((batch, N_HEADS, KN_ROWS, seq), F32)],
        compiler_params=_dense_params(),
        name="ffn0_qkv",
    )(x, fn, w_gu, w_down, kv_norm, w_k, w_vt, q_norm, w_qt)


def _oproj_ffn_final_kernel(x_ref, a_ref, wo_ref, fn_ref, wgu_ref, wd_ref, gn_ref, o_ref):
    x = x_ref[...] + lax.dot_general(a_ref[0], wo_ref[...], (((0,), (0,)), ((), ())),
                                     preferred_element_type=F32)
    y = _swiglu(x, fn_ref[...], wgu_ref, wd_ref)
    o_ref[...] = y * _rms_scale(y) * gn_ref[...]


def _oproj_ffn_final_layer(x, attn_t, w_o, fn, w_gu, w_down, final_norm):
    n = x.shape[0]
    tiles_per_seq = attn_t.shape[2] // ROW_TILE
    return pl.pallas_call(
        _oproj_ffn_final_kernel,
        grid=(n // ROW_TILE,),
        in_specs=[
            _row_spec(D_MODEL),
            pl.BlockSpec((1, D_MODEL, ROW_TILE),
                         lambda i: (i // tiles_per_seq, 0, i % tiles_per_seq)),
            _const_spec((D_MODEL, D_MODEL)),
            _const_spec((1, D_MODEL)),
            _const_spec((D_MODEL, 2 * D_FF)),
            _const_spec((D_FF, D_MODEL)),
            _const_spec((1, D_MODEL)),
        ],
        out_specs=_row_spec(D_MODEL),
        out_shape=jax.ShapeDtypeStruct((n, D_MODEL), F32),
        compiler_params=_dense_params(),
        name="oproj_ffn1_final",
    )(x, attn_t, w_o, fn, w_gu, w_down, final_norm)


def _attn_kernel(coef_ref, lam_ref, qt_ref, qall_ref, k_ref, vt_ref, kn_ref, sub_ref, o_ref,
                 kext_ref, qext_ref, mask_ref, kmax2_ref, lamrow_ref, s_ref, acc_ref, ctrl_ref):
    head = pl.program_id(1)
    qi = pl.program_id(2)
    streams = range(BATCH_BLOCK)
    c_parts = [coef_ref[head, i] for i in range(3)]
    c = coef_ref[head, 3]

    @pl.when(qi == 0)
    def _():
        lane = lax.broadcasted_iota(jnp.int32, (K_TILE, 128), 1)
        key_pos = lax.broadcasted_iota(jnp.int32, (K_TILE, 128), 0).astype(F32)
        kext = jnp.where(lane < 3, key_pos, 0.0)
        row = lax.broadcasted_iota(jnp.int32, (128, 2 * Q_TILE), 0)
        qry = lax.broadcasted_iota(jnp.int32, (128, 2 * Q_TILE), 1) % Q_TILE
        q_lo = -(qry % BF16_EXACT_INT).astype(F32)
        q_hi = -(qry - qry % BF16_EXACT_INT).astype(F32)
        qext = jnp.where((row >= 3) & (row < 6), q_lo, jnp.where((row >= 6) & (row < 9), q_hi, 0.0))
        for i in range(3):
            kext = jnp.where((lane == 3 + i) | (lane == 6 + i), c_parts[i], kext)
            qext = jnp.where(row == i, c_parts[i], qext)
        kext_ref[...] = kext.astype(BF16)
        qext_ref[...] = qext.astype(BF16)

        @pl.when((head == 0) & (pl.program_id(0) == 0))
        def _():
            key = lax.broadcasted_iota(jnp.int32, (K_TILE, Q_TILE), 0)
            qcol = lax.broadcasted_iota(jnp.int32, (K_TILE, Q_TILE), 1)
            for d in range(DIAG_TILES):
                hide = jnp.where(key + d * K_TILE <= qcol, 0.0, -jnp.inf)
                mask_ref[d] = jnp.concatenate([hide, hide], axis=1)

        lv = lam_ref[...]
        lam = (jnp.exp(jnp.sum(lv[0:1] * lv[1:2], axis=-1, keepdims=True))
               - jnp.exp(jnp.sum(lv[2:3] * lv[3:4], axis=-1, keepdims=True)) + LAMBDA_INIT)
        lamrow_ref[...] = jnp.broadcast_to(lam, (1, Q_TILE))

        kmax2 = []
        for st in streams:
            top_rows = jnp.max(kn_ref[st, 0], axis=1, keepdims=True)
            kmax2.append((top_rows[0:1], top_rows[1:2]))
            kmax2_ref[st] = jnp.concatenate([jnp.broadcast_to(top_rows[0:1], (1, Q_TILE)),
                                             jnp.broadcast_to(top_rows[1:2], (1, Q_TILE))], axis=1)

        def query_chunk(i, best):
            start = pl.multiple_of(i * NORM_CHUNK, NORM_CHUNK)
            for st in streams:
                q32 = qall_ref[st, :, pl.ds(start, NORM_CHUNK)].astype(F32)
                q32 = q32 * q32
                prod = jnp.concatenate(
                    [jnp.sum(q32[:HEAD_DIM], axis=0, keepdims=True) * kmax2[st][0],
                     jnp.sum(q32[HEAD_DIM:], axis=0, keepdims=True) * kmax2[st][1]], axis=1)
                best = jnp.maximum(best, prod)
            return best

        top = lax.fori_loop(0, qall_ref.shape[2] // NORM_CHUNK, query_chunk,
                            jnp.zeros((1, 2 * NORM_CHUNK), F32))
        spread = 2.04 * jnp.sqrt(jnp.max(top, axis=1, keepdims=True))
        reach = (spread + SKIP_MARGIN) / c
        pairs = jnp.floor(reach * (1.0 / (2 * K_TILE))) + 1.0
        ctrl_ref[0] = jnp.clip(pairs, 0.0, float(2 ** 20)).astype(jnp.int32)[0, 0]
        ctrl_ref[1] = jnp.where(spread < FIXED_REFERENCE_SPREAD, 1, 0)[0, 0]

    def query_matrix(st):
        qt = qt_ref[st]
        feat = lax.broadcasted_iota(jnp.int32, qt.shape, 0)
        zero = jnp.zeros_like(qt)
        return jnp.concatenate(
            [jnp.concatenate([jnp.where(feat < HEAD_DIM, qt, zero),
                              jnp.where(feat >= HEAD_DIM, qt, zero)], axis=1),
             qext_ref[...]], axis=0)

    def query_bounds(st):
        q32 = qt_ref[st].astype(F32)
        q32 = q32 * q32
        qn2 = jnp.concatenate([jnp.sum(q32[:HEAD_DIM], axis=0, keepdims=True),
                               jnp.sum(q32[HEAD_DIM:], axis=0, keepdims=True)], axis=1)
        return 1.02 * jnp.sqrt(qn2 * kmax2_ref[st])

    n_pairs = jnp.minimum(qi, ctrl_ref[0])
    fixed_reference_ok = ctrl_ref[1] == 1
    first = qi * DIAG_TILES

    def key_rows(st, kt):
        ks = pl.multiple_of(kt * K_TILE, K_TILE)
        return jnp.concatenate([k_ref[st, pl.ds(ks, K_TILE), :], kext_ref[...]], axis=1)

    def value_rows(st, kt):
        ks = pl.multiple_of(kt * K_TILE, K_TILE)
        return vt_ref[st, :, pl.ds(ks, K_TILE)]

    def tile_offset(kt):
        return c * (kt * K_TILE - qi * Q_TILE).astype(F32)

    @pl.when(fixed_reference_ok)
    def _():
        rhs = [query_matrix(st) for st in streams]
        bounds = [query_bounds(st) for st in streams]

        def tiles(t0, count, diagonal=False):
            kts, offs = [], []
            for i in range(count):
                if diagonal and i >= DIAG_TILES:
                    kts.append(jnp.maximum(t0 - i, 0))
                    offs.append(jnp.where(n_pairs > 0, tile_offset(kts[i]), -jnp.inf))
                else:
                    kts.append(t0 - i)
                    offs.append(tile_offset(kts[i]))
            los = [(DIAG_TILES - 1 - i) * K_TILE if diagonal and i < DIAG_TILES else 0
                   for i in range(count)]

            def narrow(a, lo):
                return a if lo == 0 else jnp.concatenate([a[:, lo:Q_TILE], a[:, Q_TILE + lo:]], axis=1)

            def widen(a, lo):
                if lo == 0:
                    return a
                gap = jnp.zeros((a.shape[0], lo), a.dtype)
                return jnp.concatenate([gap, a[:, :Q_TILE - lo], gap, a[:, Q_TILE - lo:]], axis=1)

            for st in streams:
                probs, row_sums = [], None
                for i in range(count):
                    s = jnp.dot(key_rows(st, kts[i]), narrow(rhs[st], los[i]),
                                preferred_element_type=F32)
                    if diagonal and i < DIAG_TILES:
                        s = s + narrow(mask_ref[DIAG_TILES - 1 - i], los[i])
                    p = jnp.exp2(s + (narrow(bounds[st], los[i]) + offs[i]))
                    folded = jnp.sum(p.reshape(K_TILE // SUM_PARTIALS, SUM_PARTIALS, p.shape[1]), axis=0)
                    folded = widen(folded, los[i])
                    row_sums = folded if row_sums is None else row_sums + folded
                    probs.append(p.astype(BF16))
                part = None
                whole = [i for i in range(count) if los[i] == 0]
                per_matmul = PV_TILES if diagonal else PV_TILES_FAR
                for g in range(0, len(whole), per_matmul):
                    group = whole[g:g + per_matmul]
                    values = jnp.concatenate([value_rows(st, kts[i]) for i in group], axis=1)
                    prod = jnp.dot(values, jnp.concatenate([probs[i] for i in group], axis=0),
                                   preferred_element_type=F32)
                    part = prod if part is None else part + prod
                for i in range(count):
                    if los[i] > 0:
                        prod = jnp.dot(value_rows(st, kts[i]), probs[i], preferred_element_type=F32)
                        part = part + widen(prod, los[i])
                if diagonal:
                    acc_ref[st, :V_DIM] = part
                    acc_ref[st, V_DIM:V_DIM + SUM_PARTIALS] = row_sums
                else:
                    acc_ref[st, :V_DIM] += part
                    acc_ref[st, V_DIM:V_DIM + SUM_PARTIALS] += row_sums

        tiles(first + DIAG_TILES - 1, DIAG_TILES + 2, diagonal=True)
        far_pairs = jnp.maximum(n_pairs - 1, 0)
        n_quads = far_pairs // 2
        far_start = first - 3

        def quad(j, carry):
            tiles(far_start - 4 * j, 4)
            return carry

        lax.fori_loop(0, n_quads, quad, 0)

        @pl.when(far_pairs % 2 == 1)
        def _():
            tiles(far_start - 4 * n_quads, 2)

    @pl.when(jnp.logical_not(fixed_reference_ok))
    def _():
        rhs = [query_matrix(st) for st in streams]
        acc_ref[...] = jnp.zeros_like(acc_ref)
        sum_rows = jnp.where(lax.broadcasted_iota(jnp.int32, (BF16_SUBLANES, K_TILE), 0) == 0,
                             1.0, 0.0).astype(BF16)

        def scores(st, kt, buf, diag=None):
            s = jnp.dot(key_rows(st, kt), rhs[st], preferred_element_type=F32)
            if diag is not None:
                s = s + mask_ref[diag]
            s_ref[st, buf] = s
            return jnp.max(s, axis=0, keepdims=True)

        def update(st, kt, buf, smax, m):
            off = tile_offset(kt)
            m_new = jnp.maximum(m, smax + off)
            alpha = jnp.exp2(m - m_new)
            p = jnp.exp2(s_ref[st, buf] - (m_new - off))
            lhs = jnp.concatenate([value_rows(st, kt), sum_rows], axis=0)
            acc_ref[st] = alpha * acc_ref[st] + jnp.dot(lhs, p.astype(BF16),
                                                        preferred_element_type=F32)
            return m_new

        m = [jnp.full((1, 2 * Q_TILE), -1e30, F32) for _ in streams]
        smax_b = [scores(st, first + 1, 1, diag=1) for st in streams]
        smax_a = [scores(st, first, 0, diag=0) for st in streams]
        m = [update(st, first + 1, 1, smax_b[st], m[st]) for st in streams]

        def pair(j, carry):
            smax_a, m = carry
            t = first - 2 * j
            smax_b = [scores(st, t - 1, 1) for st in streams]
            m = [update(st, t, 0, smax_a[st], m[st]) for st in streams]
            smax_a = [scores(st, t - 2, 0) for st in streams]
            m = [update(st, t - 1, 1, smax_b[st], m[st]) for st in streams]
            return smax_a, m

        smax_a, m = lax.fori_loop(0, n_pairs, pair, (smax_a, m))
        for st in streams:
            update(st, first - 2 * n_pairs, 0, smax_a[st], m[st])

    lam = lamrow_ref[...]
    for st in streams:
        acc = acc_ref[st, :V_DIM]
        sums = jnp.sum(acc_ref[st, V_DIM:V_DIM + SUM_PARTIALS], axis=0, keepdims=True)
        inv = 1.0 / sums
        o = (acc[:, :Q_TILE] * inv[:, :Q_TILE]
             - acc[:, Q_TILE:] * (lam * inv[:, Q_TILE:]))
        norm = lax.rsqrt(jnp.mean(o * o, axis=0, keepdims=True) + EPS) * (1.0 - LAMBDA_INIT)
        o_ref[st] = (o * norm * sub_ref[...]).astype(BF16)


def _alibi_coefficients():
    pieces = []
    rest = np.float64(LOG2E)
    for _ in range(3):
        piece = np.float64(np.asarray(rest, np.float32).astype(jnp.bfloat16).astype(np.float32))
        pieces.append(piece)
        rest = rest - piece
    rows = []
    for h in range(N_HEADS):
        slope = 2.0 ** (-8.0 * (h + 1) / N_HEADS)
        rows.append([slope * p for p in pieces] + [slope * sum(pieces)])
    return jnp.asarray(np.array(rows, dtype=np.float32))


def _diff_attention(lam_vecs, qt, k, vt, kn, subln):
    b, s, _ = k.shape
    return pl.pallas_call(
        _attn_kernel,
        grid=(b // BATCH_BLOCK, N_HEADS, s // Q_TILE),
        in_specs=[
            pl.BlockSpec(memory_space=pltpu.SMEM),
            pl.BlockSpec((4, HEAD_DIM), lambda bi, hi, qi: (0, 0)),
            pl.BlockSpec((BATCH_BLOCK, V_DIM, Q_TILE), lambda bi, hi, qi: (bi, hi, qi)),
            pl.BlockSpec((BATCH_BLOCK, V_DIM, s), lambda bi, hi, qi: (bi, hi, 0)),
            pl.BlockSpec((BATCH_BLOCK, s, V_DIM), lambda bi, hi, qi: (bi, 0, hi)),
            pl.BlockSpec((BATCH_BLOCK, V_DIM, s), lambda bi, hi, qi: (bi, hi, 0)),
            pl.BlockSpec((BATCH_BLOCK, 1, KN_ROWS, s), lambda bi, hi, qi: (bi, hi, 0, 0)),
            pl.BlockSpec((V_DIM, Q_TILE), lambda bi, hi, qi: (0, 0)),
        ],
        out_specs=pl.BlockSpec((BATCH_BLOCK, V_DIM, Q_TILE), lambda bi, hi, qi: (bi, hi, qi)),
        out_shape=jax.ShapeDtypeStruct((b, N_HEADS * V_DIM, s), BF16),
        scratch_shapes=[
            pltpu.VMEM((K_TILE, 128), BF16),
            pltpu.VMEM((128, 2 * Q_TILE), BF16),
            pltpu.VMEM((DIAG_TILES, K_TILE, 2 * Q_TILE), F32),
            pltpu.VMEM((BATCH_BLOCK, 1, 2 * Q_TILE), F32),
            pltpu.VMEM((1, Q_TILE), F32),
            pltpu.VMEM((BATCH_BLOCK, 2, K_TILE, 2 * Q_TILE), F32),
            pltpu.VMEM((BATCH_BLOCK, V_DIM + BF16_SUBLANES, 2 * Q_TILE), F32),
            pltpu.SMEM((2,), jnp.int32),
        ],
        compiler_params=pltpu.CompilerParams(
            dimension_semantics=("arbitrary", "arbitrary", "arbitrary"),
            vmem_limit_bytes=V7X_VMEM_LIMIT_BYTES),
        name="diff_attention",
    )(_alibi_coefficients(), lam_vecs, qt, qt, k, vt, kn, subln)


def kernel(x, a_norm, a_w_in, a_v_norm, a_w_sp, a_b_sp, a_w_out, ffn_norm, ffn_w_gu, ffn_w_down,
           kv_norm, kv_w, b_norm, b_w_q, b_lambda, b_subln, b_w_o, final_norm):
    b, s, d = x.shape
    assert d == D_MODEL and s % Q_TILE == 0 and s % ROW_TILE == 0
    assert DIAG_TILES == 2 and K_TILE <= BF16_EXACT_INT and b % BATCH_BLOCK == 0
    xs = x.reshape(b * s, d)
    row = lambda g: g.reshape(1, -1).astype(F32)

    b_full = jnp.repeat(a_b_sp[0].T.astype(F32), GROUP_DIM, axis=1)
    xs = _gmlp_layer(xs, row(a_norm[0]), a_w_in[0].astype(BF16), row(a_v_norm[0]),
                     a_w_sp[0].astype(F32), b_full, a_w_out[0].astype(BF16))

    w_k = kv_w[:, :D_MODEL].astype(BF16)
    w_vt = kv_w[:, D_MODEL:].T.astype(BF16)
    w_qt = b_w_q[0].T.astype(BF16)
    xs, k, qt, vt, kn = _ffn_qkv_layer(xs, b, row(ffn_norm[0]), ffn_w_gu[0].astype(BF16),
                                       ffn_w_down[0].astype(BF16), row(kv_norm), w_k, w_vt,
                                       row(b_norm[0]), w_qt)

    sub_cols = jnp.broadcast_to(b_subln[0].astype(F32)[:, None], (V_DIM, Q_TILE))
    attn = _diff_attention(b_lambda[0].astype(F32), qt, k.reshape(b, s, d), vt, kn, sub_cols)

    out = _oproj_ffn_final_layer(xs, attn, b_w_o[0].astype(BF16),
                                 row(ffn_norm[1]), ffn_w_gu[1].astype(BF16),
                                 ffn_w_down[1].astype(BF16), row(final_norm))
    return out.reshape(b, s, d)
```

```python
import math

import jax
import jax.numpy as jnp
import numpy as np
from jax import lax
from jax.experimental import pallas as pl
from jax.experimental.pallas import tpu as pltpu

D_MODEL = 1024
CHUNK = 128
N_GROUPS = 8
GROUP_DIM = D_MODEL // N_GROUPS
N_HEADS = 8
HEAD_DIM = 64
V_DIM = 2 * HEAD_DIM
D_FF = 2816
EPS = 1e-6
N_A = 1
LAMBDA_INIT = 0.8 - 0.6 * math.exp(-0.3 * N_A)
LOG2E = math.log2(math.e)

V7X_VMEM_LIMIT_BYTES = 56 * 1024 * 1024
BF16_EXACT_INT = 256
BF16_SUBLANES = 16

ROW_TILE = 512
GMLP_ROW_TILE = 1024
Q_TILE = 512
K_TILE = 256
DIAG_TILES = Q_TILE // K_TILE
SUM_PARTIALS = 8
NORM_CHUNK = 1024
KN_ROWS = 8
PV_TILES = 4
BATCH_BLOCK = 2
SKIP_MARGIN = 160.0
FIXED_REFERENCE_SPREAD = 100.0

F32 = jnp.float32
BF16 = jnp.bfloat16


def _rms_scale(x):
    return lax.rsqrt(jnp.mean(x * x, axis=-1, keepdims=True) + EPS)


def _gelu_tanh(x):
    c = math.sqrt(2.0 / math.pi)
    return 0.5 * x * (1.0 + jnp.tanh(c * (x + 0.044715 * (x * x * x))))


def _silu(x):
    return 0.5 * x * (1.0 + jnp.tanh(0.5 * x))


def _const_spec(shape):
    nd = len(shape)
    return pl.BlockSpec(shape, lambda *_: (0,) * nd, pipeline_mode=pl.Buffered(1))


def _row_spec(width):
    return pl.BlockSpec((ROW_TILE, width), lambda i: (i, 0))


def _dense_params():
    return pltpu.CompilerParams(dimension_semantics=("arbitrary",),
                                vmem_limit_bytes=V7X_VMEM_LIMIT_BYTES)


def _gmlp_kernel(x_ref, an_ref, win_ref, vg_ref, wsp_ref, bsp_ref, wout_ref, o_ref):
    t_idx = lax.broadcasted_iota(jnp.int32, (CHUNK, CHUNK), 0)
    s_idx = lax.broadcasted_iota(jnp.int32, (CHUNK, CHUNK), 1)
    causal = s_idx <= t_idx
    n_chunks = GMLP_ROW_TILE // CHUNK

    x = x_ref[...]
    h = (x * _rms_scale(x) * an_ref[...]).astype(BF16)
    uv = _gelu_tanh(jnp.dot(h, win_ref[...], preferred_element_type=F32))
    u = uv[:, :D_MODEL]
    v = uv[:, D_MODEL:]
    mu = jnp.mean(v, axis=-1, keepdims=True)
    vc = v - mu
    var = jnp.mean(vc * vc, axis=-1, keepdims=True)
    vn = (vc * lax.rsqrt(var + EPS) * vg_ref[...]).astype(BF16)
    cols = []
    for g in range(N_GROUPS):
        wm = jnp.where(causal, wsp_ref[g], 0.0).astype(BF16)
        bias = bsp_ref[:, g * GROUP_DIM:(g + 1) * GROUP_DIM]
        vg = jnp.concatenate([vn[c * CHUNK:(c + 1) * CHUNK, g * GROUP_DIM:(g + 1) * GROUP_DIM]
                              for c in range(n_chunks)], axis=1)
        zg = jnp.dot(wm, vg, preferred_element_type=F32)
        cols.append(jnp.concatenate([zg[:, c * GROUP_DIM:(c + 1) * GROUP_DIM] + bias
                                     for c in range(n_chunks)], axis=0))
    z = jnp.concatenate(cols, axis=1)
    gated = (u * z).astype(BF16)
    o_ref[...] = x + jnp.dot(gated, wout_ref[...], preferred_element_type=F32)


def _gmlp_layer(x, a_norm, w_in, v_gain, w_sp, b_full, w_out):
    n = x.shape[0]
    return pl.pallas_call(
        _gmlp_kernel,
        grid=(n // GMLP_ROW_TILE,),
        in_specs=[
            pl.BlockSpec((GMLP_ROW_TILE, D_MODEL), lambda i: (i, 0)),
            _const_spec((1, D_MODEL)),
            _const_spec((D_MODEL, 2 * D_MODEL)),
            _const_spec((1, D_MODEL)),
            _const_spec((N_GROUPS, CHUNK, CHUNK)),
            _const_spec((CHUNK, D_MODEL)),
            _const_spec((D_MODEL, D_MODEL)),
        ],
        out_specs=pl.BlockSpec((GMLP_ROW_TILE, D_MODEL), lambda i: (i, 0)),
        out_shape=jax.ShapeDtypeStruct((n, D_MODEL), F32),
        compiler_params=_dense_params(),
        name="gmlp_mixer",
    )(x, a_norm, w_in, v_gain, w_sp, b_full, w_out)


def _swiglu(x, fn, wgu_ref, wd_ref):
    h = (x * _rms_scale(x) * fn).astype(BF16)
    gu = jnp.dot(h, wgu_ref[...], preferred_element_type=F32)
    a = (_silu(gu[:, :D_FF]) * gu[:, D_FF:]).astype(BF16)
    return x + jnp.dot(a, wd_ref[...], preferred_element_type=F32)


def _ffn_qkv_kernel(x_ref, fn_ref, wgu_ref, wd_ref, kvn_ref, wk_ref, wvt_ref, qn_ref, wqt_ref,
                    x_out, k_out, qt_out, vt_out, kn_out):
    y = _swiglu(x_ref[...], fn_ref[...], wgu_ref, wd_ref)
    x_out[...] = y
    yn = y * _rms_scale(y)
    hk = (yn * kvn_ref[...]).astype(BF16)
    hq = (yn * qn_ref[...]).astype(BF16)
    kb = jnp.dot(hk, wk_ref[...], preferred_element_type=F32).astype(BF16)
    k_out[...] = kb
    nt = (((1,), (1,)), ((), ()))
    row = lax.broadcasted_iota(jnp.int32, (N_HEADS * KN_ROWS, D_MODEL), 0)
    col = lax.broadcasted_iota(jnp.int32, (N_HEADS * KN_ROWS, D_MODEL), 1)
    half = row % KN_ROWS
    pick = (half < 2) & (col // HEAD_DIM == 2 * (row // KN_ROWS) + half)
    sq = lax.dot_general(jnp.where(pick, 1.0, 0.0).astype(BF16), kb * kb, nt,
                         preferred_element_type=F32)
    kn_out[0] = sq.reshape(N_HEADS, KN_ROWS, sq.shape[1])
    vt_out[0] = lax.dot_general(wvt_ref[...], hk, nt, preferred_element_type=F32).astype(BF16)
    qt = lax.dot_general(wqt_ref[...], hq, nt, preferred_element_type=F32)
    qt_out[0] = (qt * (HEAD_DIM ** -0.5 * LOG2E)).astype(BF16)


def _ffn_qkv_layer(x, batch, fn, w_gu, w_down, kv_norm, w_k, w_vt, q_norm, w_qt):
    n = x.shape[0]
    seq = n // batch
    tiles_per_seq = seq // ROW_TILE
    t_spec = pl.BlockSpec((1, D_MODEL, ROW_TILE),
                          lambda i: (i // tiles_per_seq, 0, i % tiles_per_seq))
    t_shape = jax.ShapeDtypeStruct((batch, D_MODEL, seq), BF16)
    return pl.pallas_call(
        _ffn_qkv_kernel,
        grid=(n // ROW_TILE,),
        in_specs=[
            _row_spec(D_MODEL),
            _const_spec((1, D_MODEL)),
            _const_spec((D_MODEL, 2 * D_FF)),
            _const_spec((D_FF, D_MODEL)),
            _const_spec((1, D_MODEL)),
            _const_spec((D_MODEL, D_MODEL)),
            _const_spec((D_MODEL, D_MODEL)),
            _const_spec((1, D_MODEL)),
            _const_spec((D_MODEL, D_MODEL)),
        ],
        out_specs=[_row_spec(D_MODEL), _row_spec(D_MODEL), t_spec, t_spec,
                   pl.BlockSpec((1, N_HEADS, KN_ROWS, ROW_TILE),
                                lambda i: (i // tiles_per_seq, 0, 0, i % tiles_per_seq))],
        out_shape=[jax.ShapeDtypeStruct((n, D_MODEL), F32),
                   jax.ShapeDtypeStruct((n, D_MODEL), BF16), t_shape, t_shape,
                   jax.ShapeDtypeStruct((batch, N_HEADS, KN_ROWS, seq), F32)],
        compiler_params=_dense_params(),
        name="ffn0_qkv",
    )(x, fn, w_gu, w_down, kv_norm, w_k, w_vt, q_norm, w_qt)


def _oproj_ffn_final_kernel(x_ref, a_ref, wo_ref, fn_ref, wgu_ref, wd_ref, gn_ref, o_ref):
    x = x_ref[...] + lax.dot_general(a_ref[0], wo_ref[...], (((0,), (0,)), ((), ())),
                                     preferred_element_type=F32)
    y = _swiglu(x, fn_ref[...], wgu_ref, wd_ref)
    o_ref[...] = y * _rms_scale(y) * gn_ref[...]


def _oproj_ffn_final_layer(x, attn_t, w_o, fn, w_gu, w_down, final_norm):
    n = x.shape[0]
    tiles_per_seq = attn_t.shape[2] // ROW_TILE
    return pl.pallas_call(
        _oproj_ffn_final_kernel,
        grid=(n // ROW_TILE,),
        in_specs=[
            _row_spec(D_MODEL),
            pl.BlockSpec((1, D_MODEL, ROW_TILE),
                         lambda i: (i // tiles_per_seq, 0, i % tiles_per_seq)),
            _const_spec((D_MODEL, D_MODEL)),
            _const_spec((1, D_MODEL)),
            _const_spec((D_MODEL, 2 * D_FF)),
            _const_spec((D_FF, D_MODEL)),
            _const_spec((1, D_MODEL)),
        ],
        out_specs=_row_spec(D_MODEL),
        out_shape=jax.ShapeDtypeStruct((n, D_MODEL), F32),
        compiler_params=_dense_params(),
        name="oproj_ffn1_final",
    )(x, attn_t, w_o, fn, w_gu, w_down, final_norm)


def _attn_kernel(coef_ref, lam_ref, qall_ref, k_ref, vt_ref, kn_ref, sub_ref, o_ref,
                 kext_ref, qext_ref, mask_ref, kmax2_ref, lamrow_ref, s_ref, acc_ref):
    head = pl.program_id(1)
    streams = range(BATCH_BLOCK)
    c_parts = [coef_ref[head, i] for i in range(3)]
    c = coef_ref[head, 3]

    lane = lax.broadcasted_iota(jnp.int32, (K_TILE, 128), 1)
    key_pos = lax.broadcasted_iota(jnp.int32, (K_TILE, 128), 0).astype(F32)
    kext = jnp.where(lane < 3, key_pos, 0.0)
    row = lax.broadcasted_iota(jnp.int32, (128, 2 * Q_TILE), 0)
    qry = lax.broadcasted_iota(jnp.int32, (128, 2 * Q_TILE), 1) % Q_TILE
    q_lo = -(qry % BF16_EXACT_INT).astype(F32)
    q_hi = -(qry - qry % BF16_EXACT_INT).astype(F32)
    qext = jnp.where((row >= 3) & (row < 6), q_lo, jnp.where((row >= 6) & (row < 9), q_hi, 0.0))
    for i in range(3):
        kext = jnp.where((lane == 3 + i) | (lane == 6 + i), c_parts[i], kext)
        qext = jnp.where(row == i, c_parts[i], qext)
    kext_ref[...] = kext.astype(BF16)
    qext_ref[...] = qext.astype(BF16)

    @pl.when((head == 0) & (pl.program_id(0) == 0))
    def _():
        key = lax.broadcasted_iota(jnp.int32, (K_TILE, Q_TILE), 0)
        qcol = lax.broadcasted_iota(jnp.int32, (K_TILE, Q_TILE), 1)
        for d in range(DIAG_TILES):
            hide = jnp.where(key + d * K_TILE <= qcol, 0.0, -jnp.inf)
            mask_ref[d] = jnp.concatenate([hide, hide], axis=1)

    lv = lam_ref[...]
    lam = (jnp.exp(jnp.sum(lv[0:1] * lv[1:2], axis=-1, keepdims=True))
           - jnp.exp(jnp.sum(lv[2:3] * lv[3:4], axis=-1, keepdims=True)) + LAMBDA_INIT)
    lamrow_ref[...] = jnp.broadcast_to(lam, (1, Q_TILE))

    kmax2 = []
    for st in streams:
        top_rows = jnp.max(kn_ref[st, 0], axis=1, keepdims=True)
        kmax2.append((top_rows[0:1], top_rows[1:2]))
        kmax2_ref[st] = jnp.concatenate([jnp.broadcast_to(top_rows[0:1], (1, Q_TILE)),
                                         jnp.broadcast_to(top_rows[1:2], (1, Q_TILE))], axis=1)

    def query_chunk(i, best):
        start = pl.multiple_of(i * NORM_CHUNK, NORM_CHUNK)
        for st in streams:
            q32 = qall_ref[st, :, pl.ds(start, NORM_CHUNK)].astype(F32)
            q32 = q32 * q32
            prod = jnp.concatenate(
                [jnp.sum(q32[:HEAD_DIM], axis=0, keepdims=True) * kmax2[st][0],
                 jnp.sum(q32[HEAD_DIM:], axis=0, keepdims=True) * kmax2[st][1]], axis=1)
            best = jnp.maximum(best, prod)
        return best

    top = lax.fori_loop(0, qall_ref.shape[2] // NORM_CHUNK, query_chunk,
                        jnp.zeros((1, 2 * NORM_CHUNK), F32))
    spread = 2.04 * jnp.sqrt(jnp.max(top, axis=1, keepdims=True))
    reach = (spread + SKIP_MARGIN) / c
    pairs = jnp.floor(reach * (1.0 / (2 * K_TILE))) + 1.0
    max_pairs = jnp.clip(pairs, 0.0, float(2 ** 20)).astype(jnp.int32)[0, 0]
    fixed_reference_ok = jnp.where(spread < FIXED_REFERENCE_SPREAD, 1, 0)[0, 0] == 1

    def query_tile(qi, carry):
        q0 = pl.multiple_of(qi * Q_TILE, Q_TILE)

        def query_tile_rows(st):
            return qall_ref[st, :, pl.ds(q0, Q_TILE)]

        def query_matrix(st):
            qt = query_tile_rows(st)
            feat = lax.broadcasted_iota(jnp.int32, qt.shape, 0)
            zero = jnp.zeros_like(qt)
            return jnp.concatenate(
                [jnp.concatenate([jnp.where(feat < HEAD_DIM, qt, zero),
                                  jnp.where(feat >= HEAD_DIM, qt, zero)], axis=1),
                 qext_ref[...]], axis=0)

        def query_bounds(st):
            q32 = query_tile_rows(st).astype(F32)
            q32 = q32 * q32
            qn2 = jnp.concatenate([jnp.sum(q32[:HEAD_DIM], axis=0, keepdims=True),
                                   jnp.sum(q32[HEAD_DIM:], axis=0, keepdims=True)], axis=1)
            return 1.02 * jnp.sqrt(qn2 * kmax2_ref[st])

        n_pairs = jnp.minimum(qi, max_pairs)
        first = qi * DIAG_TILES

        def key_rows(st, kt):
            ks = pl.multiple_of(kt * K_TILE, K_TILE)
            return jnp.concatenate([k_ref[st, pl.ds(ks, K_TILE), :], kext_ref[...]], axis=1)

        def value_rows(st, kt):
            ks = pl.multiple_of(kt * K_TILE, K_TILE)
            return vt_ref[st, :, pl.ds(ks, K_TILE)]

        def tile_offset(kt):
            return c * (kt * K_TILE - qi * Q_TILE).astype(F32)

        @pl.when(fixed_reference_ok)
        def _():
            rhs = [query_matrix(st) for st in streams]
            bounds = [query_bounds(st) for st in streams]

            def tiles(t0, count, diagonal=False):
                kts, offs = [], []
                for i in range(count):
                    if diagonal and i >= DIAG_TILES:
                        kts.append(jnp.maximum(t0 - i, 0))
                        offs.append(jnp.where(n_pairs > 0, tile_offset(kts[i]), -jnp.inf))
                    else:
                        kts.append(t0 - i)
                        offs.append(tile_offset(kts[i]))
                los = [(DIAG_TILES - 1 - i) * K_TILE if diagonal and i < DIAG_TILES else 0
                       for i in range(count)]

                def narrow(a, lo):
                    return a if lo == 0 else jnp.concatenate([a[:, lo:Q_TILE], a[:, Q_TILE + lo:]], axis=1)

                def widen(a, lo):
                    if lo == 0:
                        return a
                    gap = jnp.zeros((a.shape[0], lo), a.dtype)
                    return jnp.concatenate([gap, a[:, :Q_TILE - lo], gap, a[:, Q_TILE - lo:]], axis=1)

                for st in streams:
                    probs, row_sums = [], None
                    for i in range(count):
                        s = jnp.dot(key_rows(st, kts[i]), narrow(rhs[st], los[i]),
                                    preferred_element_type=F32)
                        if diagonal and i < DIAG_TILES:
                            s = s + narrow(mask_ref[DIAG_TILES - 1 - i], los[i])
                        p = jnp.exp2(s + (narrow(bounds[st], los[i]) + offs[i]))
                        folded = jnp.sum(p.reshape(K_TILE // SUM_PARTIALS, SUM_PARTIALS, p.shape[1]), axis=0)
                        folded = widen(folded, los[i])
                        row_sums = folded if row_sums is None else row_sums + folded
                        probs.append(p.astype(BF16))
                    part = None
                    whole = [i for i in range(count) if los[i] == 0]
                    for g in range(0, len(whole), PV_TILES):
                        group = whole[g:g + PV_TILES]
                        values = jnp.concatenate([value_rows(st, kts[i]) for i in group], axis=1)
                        prod = jnp.dot(values, jnp.concatenate([probs[i] for i in group], axis=0),
                                       preferred_element_type=F32)
                        part = prod if part is None else part + prod
                    for i in range(count):
                        if los[i] > 0:
                            prod = jnp.dot(value_rows(st, kts[i]), probs[i], preferred_element_type=F32)
                            part = part + widen(prod, los[i])
                    if diagonal:
                        acc_ref[st, :V_DIM] = part
                        acc_ref[st, V_DIM:V_DIM + SUM_PARTIALS] = row_sums
                    else:
                        acc_ref[st, :V_DIM] += part
                        acc_ref[st, V_DIM:V_DIM + SUM_PARTIALS] += row_sums

            tiles(first + DIAG_TILES - 1, DIAG_TILES + 2, diagonal=True)
            far_pairs = jnp.maximum(n_pairs - 1, 0)
            n_quads = far_pairs // 2
            far_start = first - 3

            def quad(j, carry):
                tiles(far_start - 4 * j, 4)
                return carry

            lax.fori_loop(0, n_quads, quad, 0)

            @pl.when(far_pairs % 2 == 1)
            def _():
                tiles(far_start - 4 * n_quads, 2)

        @pl.when(jnp.logical_not(fixed_reference_ok))
        def _():
            rhs = [query_matrix(st) for st in streams]
            acc_ref[...] = jnp.zeros_like(acc_ref)
            sum_rows = jnp.where(lax.broadcasted_iota(jnp.int32, (BF16_SUBLANES, K_TILE), 0) == 0,
                                 1.0, 0.0).astype(BF16)

            def scores(st, kt, buf, diag=None):
                s = jnp.dot(key_rows(st, kt), rhs[st], preferred_element_type=F32)
                if diag is not None:
                    s = s + mask_ref[diag]
                s_ref[st, buf] = s
                return jnp.max(s, axis=0, keepdims=True)

            def update(st, kt, buf, smax, m):
                off = tile_offset(kt)
                m_new = jnp.maximum(m, smax + off)
                alpha = jnp.exp2(m - m_new)
                p = jnp.exp2(s_ref[st, buf] - (m_new - off))
                lhs = jnp.concatenate([value_rows(st, kt), sum_rows], axis=0)
                acc_ref[st] = alpha * acc_ref[st] + jnp.dot(lhs, p.astype(BF16),
                                                            preferred_element_type=F32)
                return m_new

            m = [jnp.full((1, 2 * Q_TILE), -1e30, F32) for _ in streams]
            smax_b = [scores(st, first + 1, 1, diag=1) for st in streams]
            smax_a = [scores(st, first, 0, diag=0) for st in streams]
            m = [update(st, first + 1, 1, smax_b[st], m[st]) for st in streams]

            def pair(j, carry):
                smax_a, m = carry
                t = first - 2 * j
                smax_b = [scores(st, t - 1, 1) for st in streams]
                m = [update(st, t, 0, smax_a[st], m[st]) for st in streams]
                smax_a = [scores(st, t - 2, 0) for st in streams]
                m = [update(st, t - 1, 1, smax_b[st], m[st]) for st in streams]
                return smax_a, m

            smax_a, m = lax.fori_loop(0, n_pairs, pair, (smax_a, m))
            for st in streams:
                update(st, first - 2 * n_pairs, 0, smax_a[st], m[st])

        lam = lamrow_ref[...]
        for st in streams:
            acc = acc_ref[st, :V_DIM]
            sums = jnp.sum(acc_ref[st, V_DIM:V_DIM + SUM_PARTIALS], axis=0, keepdims=True)
            inv = 1.0 / sums
            o = (acc[:, :Q_TILE] * inv[:, :Q_TILE]
                 - acc[:, Q_TILE:] * (lam * inv[:, Q_TILE:]))
            norm = lax.rsqrt(jnp.mean(o * o, axis=0, keepdims=True) + EPS) * (1.0 - LAMBDA_INIT)
            o_ref[st, :, pl.ds(q0, Q_TILE)] = (o * norm * sub_ref[...]).astype(BF16)
        return carry

    lax.fori_loop(0, qall_ref.shape[2] // Q_TILE, query_tile, 0)


def _alibi_coefficients():
    pieces = []
    rest = np.float64(LOG2E)
    for _ in range(3):
        piece = np.float64(np.asarray(rest, np.float32).astype(jnp.bfloat16).astype(np.float32))
        pieces.append(piece)
        rest = rest - piece
    rows = []
    for h in range(N_HEADS):
        slope = 2.0 ** (-8.0 * (h + 1) / N_HEADS)
        rows.append([slope * p for p in pieces] + [slope * sum(pieces)])
    return jnp.asarray(np.array(rows, dtype=np.float32))


def _diff_attention(lam_vecs, qt, k, vt, kn, subln):
    b, s, _ = k.shape
    return pl.pallas_call(
        _attn_kernel,
        grid=(b // BATCH_BLOCK, N_HEADS),
        in_specs=[
            pl.BlockSpec(memory_space=pltpu.SMEM),
            pl.BlockSpec((4, HEAD_DIM), lambda bi, hi: (0, 0)),
            pl.BlockSpec((BATCH_BLOCK, V_DIM, s), lambda bi, hi: (bi, hi, 0)),
            pl.BlockSpec((BATCH_BLOCK, s, V_DIM), lambda bi, hi: (bi, 0, hi)),
            pl.BlockSpec((BATCH_BLOCK, V_DIM, s), lambda bi, hi: (bi, hi, 0)),
            pl.BlockSpec((BATCH_BLOCK, 1, KN_ROWS, s), lambda bi, hi: (bi, hi, 0, 0)),
            pl.BlockSpec((V_DIM, Q_TILE), lambda bi, hi: (0, 0)),
        ],
        out_specs=pl.BlockSpec((BATCH_BLOCK, V_DIM, s), lambda bi, hi: (bi, hi, 0)),
        out_shape=jax.ShapeDtypeStruct((b, N_HEADS * V_DIM, s), BF16),
        scratch_shapes=[
            pltpu.VMEM((K_TILE, 128), BF16),
            pltpu.VMEM((128, 2 * Q_TILE), BF16),
            pltpu.VMEM((DIAG_TILES, K_TILE, 2 * Q_TILE), F32),
            pltpu.VMEM((BATCH_BLOCK, 1, 2 * Q_TILE), F32),
            pltpu.VMEM((1, Q_TILE), F32),
            pltpu.VMEM((BATCH_BLOCK, 2, K_TILE, 2 * Q_TILE), F32),
            pltpu.VMEM((BATCH_BLOCK, V_DIM + BF16_SUBLANES, 2 * Q_TILE), F32),
        ],
        compiler_params=pltpu.CompilerParams(
            dimension_semantics=("arbitrary", "arbitrary"),
            vmem_limit_bytes=V7X_VMEM_LIMIT_BYTES),
        name="diff_attention",
    )(_alibi_coefficients(), lam_vecs, qt, k, vt, kn, subln)


def kernel(x, a_norm, a_w_in, a_v_norm, a_w_sp, a_b_sp, a_w_out, ffn_norm, ffn_w_gu, ffn_w_down,
           kv_norm, kv_w, b_norm, b_w_q, b_lambda, b_subln, b_w_o, final_norm):
    b, s, d = x.shape
    assert d == D_MODEL and s % Q_TILE == 0 and s % ROW_TILE == 0
    assert DIAG_TILES == 2 and K_TILE <= BF16_EXACT_INT and b % BATCH_BLOCK == 0
    xs = x.reshape(b * s, d)
    row = lambda g: g.reshape(1, -1).astype(F32)

    b_full = jnp.repeat(a_b_sp[0].T.astype(F32), GROUP_DIM, axis=1)
    xs = _gmlp_layer(xs, row(a_norm[0]), a_w_in[0].astype(BF16), row(a_v_norm[0]),
                     a_w_sp[0].astype(F32), b_full, a_w_out[0].astype(BF16))

    w_k = kv_w[:, :D_MODEL].astype(BF16)
    w_vt = kv_w[:, D_MODEL:].T.astype(BF16)
    w_qt = b_w_q[0].T.astype(BF16)
    xs, k, qt, vt, kn = _ffn_qkv_layer(xs, b, row(ffn_norm[0]), ffn_w_gu[0].astype(BF16),
                                       ffn_w_down[0].astype(BF16), row(kv_norm), w_k, w_vt,
                                       row(b_norm[0]), w_qt)

    sub_cols = jnp.broadcast_to(b_subln[0].astype(F32)[:, None], (V_DIM, Q_TILE))
    attn = _diff_attention(b_lambda[0].astype(F32), qt, k.reshape(b, s, d), vt, kn, sub_cols)

    out = _oproj_ffn_final_layer(xs, attn, b_w_o[0].astype(BF16),
                                 row(ffn_norm[1]), ffn_w_gu[1].astype(BF16),
                                 ffn_w_down[1].astype(BF16), row(final_norm))
    return out.reshape(b, s, d)
```

```python
import math

import jax
import jax.numpy as jnp
import numpy as np
from jax import lax
from jax.experimental import pallas as pl
from jax.experimental.pallas import tpu as pltpu

D_MODEL = 1024
CHUNK = 128
N_GROUPS = 8
GROUP_DIM = D_MODEL // N_GROUPS
N_HEADS = 8
HEAD_DIM = 64
V_DIM = 2 * HEAD_DIM
D_FF = 2816
EPS = 1e-6
N_A = 1
LAMBDA_INIT = 0.8 - 0.6 * math.exp(-0.3 * N_A)
LOG2E = math.log2(math.e)

V7X_VMEM_LIMIT_BYTES = 56 * 1024 * 1024
BF16_EXACT_INT = 256
BF16_SUBLANES = 16

ROW_TILE = 512
GMLP_ROW_TILE = 1024
Q_TILE = 512
K_TILE = 256
DIAG_TILES = Q_TILE // K_TILE
SUM_PARTIALS = 8
NORM_CHUNK = 1024
KN_ROWS = 8
PV_TILES = 4
BATCH_BLOCK = 2
SKIP_MARGIN = 160.0
FIXED_REFERENCE_SPREAD = 100.0

F32 = jnp.float32
BF16 = jnp.bfloat16


def _rms_scale(x):
    return lax.rsqrt(jnp.mean(x * x, axis=-1, keepdims=True) + EPS)


def _gelu_tanh(x):
    c = math.sqrt(2.0 / math.pi)
    return 0.5 * x * (1.0 + jnp.tanh(c * (x + 0.044715 * (x * x * x))))


def _silu(x):
    return 0.5 * x * (1.0 + jnp.tanh(0.5 * x))


def _const_spec(shape):
    nd = len(shape)
    return pl.BlockSpec(shape, lambda *_: (0,) * nd, pipeline_mode=pl.Buffered(1))


def _row_spec(width):
    return pl.BlockSpec((ROW_TILE, width), lambda i: (i, 0))


def _dense_params():
    return pltpu.CompilerParams(dimension_semantics=("arbitrary",),
                                vmem_limit_bytes=V7X_VMEM_LIMIT_BYTES)


def _gmlp_kernel(x_ref, an_ref, win_ref, vg_ref, wsp_ref, bsp_ref, wout_ref, o_ref):
    t_idx = lax.broadcasted_iota(jnp.int32, (CHUNK, CHUNK), 0)
    s_idx = lax.broadcasted_iota(jnp.int32, (CHUNK, CHUNK), 1)
    causal = s_idx <= t_idx
    n_chunks = GMLP_ROW_TILE // CHUNK

    x = x_ref[...]
    h = (x * _rms_scale(x) * an_ref[...]).astype(BF16)
    uv = _gelu_tanh(jnp.dot(h, win_ref[...], preferred_element_type=F32))
    u = uv[:, :D_MODEL]
    v = uv[:, D_MODEL:]
    mu = jnp.mean(v, axis=-1, keepdims=True)
    vc = v - mu
    var = jnp.mean(vc * vc, axis=-1, keepdims=True)
    vn = (vc * lax.rsqrt(var + EPS) * vg_ref[...]).astype(BF16)
    cols = []
    for g in range(N_GROUPS):
        wm = jnp.where(causal, wsp_ref[g], 0.0).astype(BF16)
        bias = bsp_ref[:, g * GROUP_DIM:(g + 1) * GROUP_DIM]
        vg = jnp.concatenate([vn[c * CHUNK:(c + 1) * CHUNK, g * GROUP_DIM:(g + 1) * GROUP_DIM]
                              for c in range(n_chunks)], axis=1)
        zg = jnp.dot(wm, vg, preferred_element_type=F32)
        cols.append(jnp.concatenate([zg[:, c * GROUP_DIM:(c + 1) * GROUP_DIM] + bias
                                     for c in range(n_chunks)], axis=0))
    z = jnp.concatenate(cols, axis=1)
    gated = (u * z).astype(BF16)
    o_ref[...] = x + jnp.dot(gated, wout_ref[...], preferred_element_type=F32)


def _gmlp_layer(x, a_norm, w_in, v_gain, w_sp, b_full, w_out):
    n = x.shape[0]
    return pl.pallas_call(
        _gmlp_kernel,
        grid=(n // GMLP_ROW_TILE,),
        in_specs=[
            pl.BlockSpec((GMLP_ROW_TILE, D_MODEL), lambda i: (i, 0)),
            _const_spec((1, D_MODEL)),
            _const_spec((D_MODEL, 2 * D_MODEL)),
            _const_spec((1, D_MODEL)),
            _const_spec((N_GROUPS, CHUNK, CHUNK)),
            _const_spec((CHUNK, D_MODEL)),
            _const_spec((D_MODEL, D_MODEL)),
        ],
        out_specs=pl.BlockSpec((GMLP_ROW_TILE, D_MODEL), lambda i: (i, 0)),
        out_shape=jax.ShapeDtypeStruct((n, D_MODEL), F32),
        compiler_params=_dense_params(),
        name="gmlp_mixer",
    )(x, a_norm, w_in, v_gain, w_sp, b_full, w_out)


def _swiglu(x, fn, wgu_ref, wd_ref):
    h = (x * _rms_scale(x) * fn).astype(BF16)
    gu = jnp.dot(h, wgu_ref[...], preferred_element_type=F32)
    a = (_silu(gu[:, :D_FF]) * gu[:, D_FF:]).astype(BF16)
    return x + jnp.dot(a, wd_ref[...], preferred_element_type=F32)


def _ffn_qkv_kernel(x_ref, fn_ref, wgu_ref, wd_ref, kvn_ref, wk_ref, wvt_ref, qn_ref, wqt_ref,
                    x_out, k_out, qt_out, vt_out, kn_out):
    y = _swiglu(x_ref[...], fn_ref[...], wgu_ref, wd_ref)
    x_out[...] = y
    yn = y * _rms_scale(y)
    hk = (yn * kvn_ref[...]).astype(BF16)
    hq = (yn * qn_ref[...]).astype(BF16)
    kb = jnp.dot(hk, wk_ref[...], preferred_element_type=F32).astype(BF16)
    k_out[...] = kb
    nt = (((1,), (1,)), ((), ()))
    row = lax.broadcasted_iota(jnp.int32, (N_HEADS * KN_ROWS, D_MODEL), 0)
    col = lax.broadcasted_iota(jnp.int32, (N_HEADS * KN_ROWS, D_MODEL), 1)
    half = row % KN_ROWS
    pick = (half < 2) & (col // HEAD_DIM == 2 * (row // KN_ROWS) + half)
    sq = lax.dot_general(jnp.where(pick, 1.0, 0.0).astype(BF16), kb * kb, nt,
                         preferred_element_type=F32)
    kn_out[0] = sq.reshape(N_HEADS, KN_ROWS, sq.shape[1])
    vt_out[0] = lax.dot_general(wvt_ref[...], hk, nt, preferred_element_type=F32).astype(BF16)
    qt = lax.dot_general(wqt_ref[...], hq, nt, preferred_element_type=F32)
    qt_out[0] = (qt * (HEAD_DIM ** -0.5 * LOG2E)).astype(BF16)


def _ffn_qkv_layer(x, batch, fn, w_gu, w_down, kv_norm, w_k, w_vt, q_norm, w_qt):
    n = x.shape[0]
    seq = n // batch
    tiles_per_seq = seq // ROW_TILE
    t_spec = pl.BlockSpec((1, D_MODEL, ROW_TILE),
                          lambda i: (i // tiles_per_seq, 0, i % tiles_per_seq))
    t_shape = jax.ShapeDtypeStruct((batch, D_MODEL, seq), BF16)
    return pl.pallas_call(
        _ffn_qkv_kernel,
        grid=(n // ROW_TILE,),
        in_specs=[
            _row_spec(D_MODEL),
            _const_spec((1, D_MODEL)),
            _const_spec((D_MODEL, 2 * D_FF)),
            _const_spec((D_FF, D_MODEL)),
            _const_spec((1, D_MODEL)),
            _const_spec((D_MODEL, D_MODEL)),
            _const_spec((D_MODEL, D_MODEL)),
            _const_spec((1, D_MODEL)),
            _const_spec((D_MODEL, D_MODEL)),
        ],
        out_specs=[_row_spec(D_MODEL), _row_spec(D_MODEL), t_spec, t_spec,
                   pl.BlockSpec((1, N_HEADS, KN_ROWS, ROW_TILE),
                                lambda i: (i // tiles_per_seq, 0, 0, i % tiles_per_seq))],
        out_shape=[jax.ShapeDtypeStruct((n, D_MODEL), F32),
                   jax.ShapeDtypeStruct((n, D_MODEL), BF16), t_shape, t_shape,
                   jax.ShapeDtypeStruct((batch, N_HEADS, KN_ROWS, seq), F32)],
        compiler_params=_dense_params(),
        name="ffn0_qkv",
    )(x, fn, w_gu, w_down, kv_norm, w_k, w_vt, q_norm, w_qt)


def _oproj_ffn_final_kernel(x_ref, a_ref, wo_ref, fn_ref, wgu_ref, wd_ref, gn_ref, o_ref):
    x = x_ref[...] + lax.dot_general(a_ref[0], wo_ref[...], (((0,), (0,)), ((), ())),
                                     preferred_element_type=F32)
    y = _swiglu(x, fn_ref[...], wgu_ref, wd_ref)
    o_ref[...] = y * _rms_scale(y) * gn_ref[...]


def _oproj_ffn_final_layer(x, attn_t, w_o, fn, w_gu, w_down, final_norm):
    n = x.shape[0]
    tiles_per_seq = attn_t.shape[2] // ROW_TILE
    return pl.pallas_call(
        _oproj_ffn_final_kernel,
        grid=(n // ROW_TILE,),
        in_specs=[
            _row_spec(D_MODEL),
            pl.BlockSpec((1, D_MODEL, ROW_TILE),
                         lambda i: (i // tiles_per_seq, 0, i % tiles_per_seq)),
            _const_spec((D_MODEL, D_MODEL)),
            _const_spec((1, D_MODEL)),
            _const_spec((D_MODEL, 2 * D_FF)),
            _const_spec((D_FF, D_MODEL)),
            _const_spec((1, D_MODEL)),
        ],
        out_specs=_row_spec(D_MODEL),
        out_shape=jax.ShapeDtypeStruct((n, D_MODEL), F32),
        compiler_params=_dense_params(),
        name="oproj_ffn1_final",
    )(x, attn_t, w_o, fn, w_gu, w_down, final_norm)


def _attn_kernel(coef_ref, lam_ref, qall_ref, k_ref, vt_ref, kn_ref, sub_ref, o_ref,
                 kext_ref, qext_ref, mask_ref, kmax2_ref, lamrow_ref, s_ref, acc_ref):
    head = pl.program_id(1)
    streams = range(BATCH_BLOCK)
    c_parts = [coef_ref[head, i] for i in range(3)]
    c = coef_ref[head, 3]

    lane = lax.broadcasted_iota(jnp.int32, (K_TILE, 128), 1)
    key_pos = lax.broadcasted_iota(jnp.int32, (K_TILE, 128), 0).astype(F32)
    kext = jnp.where(lane < 3, key_pos, 0.0)
    row = lax.broadcasted_iota(jnp.int32, (128, 2 * Q_TILE), 0)
    qry = lax.broadcasted_iota(jnp.int32, (128, 2 * Q_TILE), 1) % Q_TILE
    q_lo = -(qry % BF16_EXACT_INT).astype(F32)
    q_hi = -(qry - qry % BF16_EXACT_INT).astype(F32)
    qext = jnp.where((row >= 3) & (row < 6), q_lo, jnp.where((row >= 6) & (row < 9), q_hi, 0.0))
    for i in range(3):
        kext = jnp.where((lane == 3 + i) | (lane == 6 + i), c_parts[i], kext)
        qext = jnp.where(row == i, c_parts[i], qext)
    kext_ref[...] = kext.astype(BF16)
    qext_ref[...] = qext.astype(BF16)

    @pl.when((head == 0) & (pl.program_id(0) == 0))
    def _():
        key = lax.broadcasted_iota(jnp.int32, (K_TILE, Q_TILE), 0)
        qcol = lax.broadcasted_iota(jnp.int32, (K_TILE, Q_TILE), 1)
        for d in range(DIAG_TILES):
            hide = jnp.where(key + d * K_TILE <= qcol, 0.0, -jnp.inf)
            mask_ref[d] = jnp.concatenate([hide, hide], axis=1)

    lv = lam_ref[...]
    lam = (jnp.exp(jnp.sum(lv[0:1] * lv[1:2], axis=-1, keepdims=True))
           - jnp.exp(jnp.sum(lv[2:3] * lv[3:4], axis=-1, keepdims=True)) + LAMBDA_INIT)
    lamrow_ref[...] = jnp.broadcast_to(lam, (1, Q_TILE))

    kmax2 = []
    for st in streams:
        top_rows = jnp.max(kn_ref[st, 0], axis=1, keepdims=True)
        kmax2.append((top_rows[0:1], top_rows[1:2]))
        kmax2_ref[st] = jnp.concatenate([jnp.broadcast_to(top_rows[0:1], (1, Q_TILE)),
                                         jnp.broadcast_to(top_rows[1:2], (1, Q_TILE))], axis=1)

    def query_chunk(i, best):
        start = pl.multiple_of(i * NORM_CHUNK, NORM_CHUNK)
        for st in streams:
            q32 = qall_ref[st, :, pl.ds(start, NORM_CHUNK)].astype(F32)
            q32 = q32 * q32
            prod = jnp.concatenate(
                [jnp.sum(q32[:HEAD_DIM], axis=0, keepdims=True) * kmax2[st][0],
                 jnp.sum(q32[HEAD_DIM:], axis=0, keepdims=True) * kmax2[st][1]], axis=1)
            best = jnp.maximum(best, prod)
        return best

    top = lax.fori_loop(0, qall_ref.shape[2] // NORM_CHUNK, query_chunk,
                        jnp.zeros((1, 2 * NORM_CHUNK), F32))
    spread = 2.04 * jnp.sqrt(jnp.max(top, axis=1, keepdims=True))
    reach = (spread + SKIP_MARGIN) / c
    pairs = jnp.floor(reach * (1.0 / (2 * K_TILE))) + 1.0
    max_pairs = jnp.clip(pairs, 0.0, float(2 ** 20)).astype(jnp.int32)[0, 0]
    fixed_reference_ok = jnp.where(spread < FIXED_REFERENCE_SPREAD, 1, 0)[0, 0] == 1

    def query_tile(qi, carry):
        qi = jnp.asarray(qi, jnp.int32)
        q0 = pl.multiple_of(qi * Q_TILE, Q_TILE)

        def query_tile_rows(st):
            return qall_ref[st, :, pl.ds(q0, Q_TILE)]

        def query_matrix(st):
            qt = query_tile_rows(st)
            feat = lax.broadcasted_iota(jnp.int32, qt.shape, 0)
            zero = jnp.zeros_like(qt)
            return jnp.concatenate(
                [jnp.concatenate([jnp.where(feat < HEAD_DIM, qt, zero),
                                  jnp.where(feat >= HEAD_DIM, qt, zero)], axis=1),
                 qext_ref[...]], axis=0)

        def query_bounds(st):
            q32 = query_tile_rows(st).astype(F32)
            q32 = q32 * q32
            qn2 = jnp.concatenate([jnp.sum(q32[:HEAD_DIM], axis=0, keepdims=True),
                                   jnp.sum(q32[HEAD_DIM:], axis=0, keepdims=True)], axis=1)
            return 1.02 * jnp.sqrt(qn2 * kmax2_ref[st])

        n_pairs = jnp.minimum(qi, max_pairs)
        first = qi * DIAG_TILES

        def key_rows(st, kt):
            ks = pl.multiple_of(kt * K_TILE, K_TILE)
            return jnp.concatenate([k_ref[st, pl.ds(ks, K_TILE), :], kext_ref[...]], axis=1)

        def value_rows(st, kt):
            ks = pl.multiple_of(kt * K_TILE, K_TILE)
            return vt_ref[st, :, pl.ds(ks, K_TILE)]

        def tile_offset(kt):
            return c * (kt * K_TILE - qi * Q_TILE).astype(F32)

        @pl.when(fixed_reference_ok)
        def _():
            rhs = [query_matrix(st) for st in streams]
            bounds = [query_bounds(st) for st in streams]

            def tiles(t0, count, diagonal=False):
                kts, offs = [], []
                for i in range(count):
                    if diagonal and i >= DIAG_TILES:
                        kts.append(jnp.maximum(t0 - i, 0))
                        offs.append(jnp.where(n_pairs > 0, tile_offset(kts[i]), -jnp.inf))
                    else:
                        kts.append(t0 - i)
                        offs.append(tile_offset(kts[i]))
                los = [(DIAG_TILES - 1 - i) * K_TILE if diagonal and i < DIAG_TILES else 0
                       for i in range(count)]

                def narrow(a, lo):
                    return a if lo == 0 else jnp.concatenate([a[:, lo:Q_TILE], a[:, Q_TILE + lo:]], axis=1)

                def widen(a, lo):
                    if lo == 0:
                        return a
                    gap = jnp.zeros((a.shape[0], lo), a.dtype)
                    return jnp.concatenate([gap, a[:, :Q_TILE - lo], gap, a[:, Q_TILE - lo:]], axis=1)

                for st in streams:
                    probs, row_sums = [], None
                    for i in range(count):
                        s = jnp.dot(key_rows(st, kts[i]), narrow(rhs[st], los[i]),
                                    preferred_element_type=F32)
                        if diagonal and i < DIAG_TILES:
                            s = s + narrow(mask_ref[DIAG_TILES - 1 - i], los[i])
                        p = jnp.exp2(s + (narrow(bounds[st], los[i]) + offs[i]))
                        folded = jnp.sum(p.reshape(K_TILE // SUM_PARTIALS, SUM_PARTIALS, p.shape[1]), axis=0)
                        folded = widen(folded, los[i])
                        row_sums = folded if row_sums is None else row_sums + folded
                        probs.append(p.astype(BF16))
                    part = None
                    whole = [i for i in range(count) if los[i] == 0]
                    for g in range(0, len(whole), PV_TILES):
                        group = whole[g:g + PV_TILES]
                        values = jnp.concatenate([value_rows(st, kts[i]) for i in group], axis=1)
                        prod = jnp.dot(values, jnp.concatenate([probs[i] for i in group], axis=0),
                                       preferred_element_type=F32)
                        part = prod if part is None else part + prod
                    for i in range(count):
                        if los[i] > 0:
                            prod = jnp.dot(value_rows(st, kts[i]), probs[i], preferred_element_type=F32)
                            part = part + widen(prod, los[i])
                    if diagonal:
                        acc_ref[st, :V_DIM] = part
                        acc_ref[st, V_DIM:V_DIM + SUM_PARTIALS] = row_sums
                    else:
                        acc_ref[st, :V_DIM] += part
                        acc_ref[st, V_DIM:V_DIM + SUM_PARTIALS] += row_sums

            tiles(first + DIAG_TILES - 1, DIAG_TILES + 2, diagonal=True)
            far_pairs = jnp.maximum(n_pairs - 1, 0)
            n_quads = far_pairs // 2
            far_start = first - 3

            def quad(j, carry):
                tiles(far_start - 4 * j, 4)
                return carry

            lax.fori_loop(0, n_quads, quad, 0)

            @pl.when(far_pairs % 2 == 1)
            def _():
                tiles(far_start - 4 * n_quads, 2)

        @pl.when(jnp.logical_not(fixed_reference_ok))
        def _():
            rhs = [query_matrix(st) for st in streams]
            acc_ref[...] = jnp.zeros_like(acc_ref)
            sum_rows = jnp.where(lax.broadcasted_iota(jnp.int32, (BF16_SUBLANES, K_TILE), 0) == 0,
                                 1.0, 0.0).astype(BF16)

            def scores(st, kt, buf, diag=None):
                s = jnp.dot(key_rows(st, kt), rhs[st], preferred_element_type=F32)
                if diag is not None:
                    s = s + mask_ref[diag]
                s_ref[st, buf] = s
                return jnp.max(s, axis=0, keepdims=True)

            def update(st, kt, buf, smax, m):
                off = tile_offset(kt)
                m_new = jnp.maximum(m, smax + off)
                alpha = jnp.exp2(m - m_new)
                p = jnp.exp2(s_ref[st, buf] - (m_new - off))
                lhs = jnp.concatenate([value_rows(st, kt), sum_rows], axis=0)
                acc_ref[st] = alpha * acc_ref[st] + jnp.dot(lhs, p.astype(BF16),
                                                            preferred_element_type=F32)
                return m_new

            m = [jnp.full((1, 2 * Q_TILE), -1e30, F32) for _ in streams]
            smax_b = [scores(st, first + 1, 1, diag=1) for st in streams]
            smax_a = [scores(st, first, 0, diag=0) for st in streams]
            m = [update(st, first + 1, 1, smax_b[st], m[st]) for st in streams]

            def pair(j, carry):
                smax_a, m = carry
                t = first - 2 * j
                smax_b = [scores(st, t - 1, 1) for st in streams]
                m = [update(st, t, 0, smax_a[st], m[st]) for st in streams]
                smax_a = [scores(st, t - 2, 0) for st in streams]
                m = [update(st, t - 1, 1, smax_b[st], m[st]) for st in streams]
                return smax_a, m

            smax_a, m = lax.fori_loop(0, n_pairs, pair, (smax_a, m))
            for st in streams:
                update(st, first - 2 * n_pairs, 0, smax_a[st], m[st])

        lam = lamrow_ref[...]
        for st in streams:
            acc = acc_ref[st, :V_DIM]
            sums = jnp.sum(acc_ref[st, V_DIM:V_DIM + SUM_PARTIALS], axis=0, keepdims=True)
            inv = 1.0 / sums
            o = (acc[:, :Q_TILE] * inv[:, :Q_TILE]
                 - acc[:, Q_TILE:] * (lam * inv[:, Q_TILE:]))
            norm = lax.rsqrt(jnp.mean(o * o, axis=0, keepdims=True) + EPS) * (1.0 - LAMBDA_INIT)
            o_ref[st, :, pl.ds(q0, Q_TILE)] = (o * norm * sub_ref[...]).astype(BF16)
        return carry

    lax.fori_loop(0, qall_ref.shape[2] // Q_TILE, query_tile, 0)


def _alibi_coefficients():
    pieces = []
    rest = np.float64(LOG2E)
    for _ in range(3):
        piece = np.float64(np.asarray(rest, np.float32).astype(jnp.bfloat16).astype(np.float32))
        pieces.append(piece)
        rest = rest - piece
    rows = []
    for h in range(N_HEADS):
        slope = 2.0 ** (-8.0 * (h + 1) / N_HEADS)
        rows.append([slope * p for p in pieces] + [slope * sum(pieces)])
    return jnp.asarray(np.array(rows, dtype=np.float32))


def _diff_attention(lam_vecs, qt, k, vt, kn, subln):
    b, s, _ = k.shape
    return pl.pallas_call(
        _attn_kernel,
        grid=(b // BATCH_BLOCK, N_HEADS),
        in_specs=[
            pl.BlockSpec(memory_space=pltpu.SMEM),
            pl.BlockSpec((4, HEAD_DIM), lambda bi, hi: (0, 0)),
            pl.BlockSpec((BATCH_BLOCK, V_DIM, s), lambda bi, hi: (bi, hi, 0)),
            pl.BlockSpec((BATCH_BLOCK, s, V_DIM), lambda bi, hi: (bi, 0, hi)),
            pl.BlockSpec((BATCH_BLOCK, V_DIM, s), lambda bi, hi: (bi, hi, 0)),
            pl.BlockSpec((BATCH_BLOCK, 1, KN_ROWS, s), lambda bi, hi: (bi, hi, 0, 0)),
            pl.BlockSpec((V_DIM, Q_TILE), lambda bi, hi: (0, 0)),
        ],
        out_specs=pl.BlockSpec((BATCH_BLOCK, V_DIM, s), lambda bi, hi: (bi, hi, 0)),
        out_shape=jax.ShapeDtypeStruct((b, N_HEADS * V_DIM, s), BF16),
        scratch_shapes=[
            pltpu.VMEM((K_TILE, 128), BF16),
            pltpu.VMEM((128, 2 * Q_TILE), BF16),
            pltpu.VMEM((DIAG_TILES, K_TILE, 2 * Q_TILE), F32),
            pltpu.VMEM((BATCH_BLOCK, 1, 2 * Q_TILE), F32),
            pltpu.VMEM((1, Q_TILE), F32),
            pltpu.VMEM((BATCH_BLOCK, 2, K_TILE, 2 * Q_TILE), F32),
            pltpu.VMEM((BATCH_BLOCK, V_DIM + BF16_SUBLANES, 2 * Q_TILE), F32),
        ],
        compiler_params=pltpu.CompilerParams(
            dimension_semantics=("arbitrary", "arbitrary"),
            vmem_limit_bytes=V7X_VMEM_LIMIT_BYTES),
        name="diff_attention",
    )(_alibi_coefficients(), lam_vecs, qt, k, vt, kn, subln)


def kernel(x, a_norm, a_w_in, a_v_norm, a_w_sp, a_b_sp, a_w_out, ffn_norm, ffn_w_gu, ffn_w_down,
           kv_norm, kv_w, b_norm, b_w_q, b_lambda, b_subln, b_w_o, final_norm):
    b, s, d = x.shape
    assert d == D_MODEL and s % Q_TILE == 0 and s % ROW_TILE == 0
    assert DIAG_TILES == 2 and K_TILE <= BF16_EXACT_INT and b % BATCH_BLOCK == 0
    xs = x.reshape(b * s, d)
    row = lambda g: g.reshape(1, -1).astype(F32)

    b_full = jnp.repeat(a_b_sp[0].T.astype(F32), GROUP_DIM, axis=1)
    xs = _gmlp_layer(xs, row(a_norm[0]), a_w_in[0].astype(BF16), row(a_v_norm[0]),
                     a_w_sp[0].astype(F32), b_full, a_w_out[0].astype(BF16))

    w_k = kv_w[:, :D_MODEL].astype(BF16)
    w_vt = kv_w[:, D_MODEL:].T.astype(BF16)
    w_qt = b_w_q[0].T.astype(BF16)
    xs, k, qt, vt, kn = _ffn_qkv_layer(xs, b, row(ffn_norm[0]), ffn_w_gu[0].astype(BF16),
                                       ffn_w_down[0].astype(BF16), row(kv_norm), w_k, w_vt,
                                       row(b_norm[0]), w_qt)

    sub_cols = jnp.broadcast_to(b_subln[0].astype(F32)[:, None], (V_DIM, Q_TILE))
    attn = _diff_attention(b_lambda[0].astype(F32), qt, k.reshape(b, s, d), vt, kn, sub_cols)

    out = _oproj_ffn_final_layer(xs, attn, b_w_o[0].astype(BF16),
                                 row(ffn_norm[1]), ffn_w_gu[1].astype(BF16),
                                 ffn_w_down[1].astype(BF16), row(final_norm))
    return out.reshape(b, s, d)
```

```python
import math

import jax
import jax.numpy as jnp
import numpy as np
from jax import lax
from jax.experimental import pallas as pl
from jax.experimental.pallas import tpu as pltpu

D_MODEL = 1024
CHUNK = 128
N_GROUPS = 8
GROUP_DIM = D_MODEL // N_GROUPS
N_HEADS = 8
HEAD_DIM = 64
V_DIM = 2 * HEAD_DIM
D_FF = 2816
EPS = 1e-6
N_A = 1
LAMBDA_INIT = 0.8 - 0.6 * math.exp(-0.3 * N_A)
LOG2E = math.log2(math.e)

V7X_VMEM_LIMIT_BYTES = 56 * 1024 * 1024
BF16_EXACT_INT = 256
BF16_SUBLANES = 16

ROW_TILE = 512
Q_TILE = 512
K_TILE = 256
DIAG_TILES = Q_TILE // K_TILE
SUM_PARTIALS = 8
NORM_CHUNK = 1024
KN_ROWS = 8
PV_TILES = 4
BATCH_BLOCK = 2
SKIP_MARGIN = 160.0
FIXED_REFERENCE_SPREAD = 100.0

F32 = jnp.float32
BF16 = jnp.bfloat16


def _rms_scale(x):
    return lax.rsqrt(jnp.mean(x * x, axis=-1, keepdims=True) + EPS)


def _gelu_tanh(x):
    c = math.sqrt(2.0 / math.pi)
    return 0.5 * x * (1.0 + jnp.tanh(c * (x + 0.044715 * (x * x * x))))


def _silu(x):
    return 0.5 * x * (1.0 + jnp.tanh(0.5 * x))


def _const_spec(shape):
    nd = len(shape)
    return pl.BlockSpec(shape, lambda *_: (0,) * nd, pipeline_mode=pl.Buffered(1))


def _row_spec(width):
    return pl.BlockSpec((ROW_TILE, width), lambda i: (i, 0))


def _dense_params():
    return pltpu.CompilerParams(dimension_semantics=("arbitrary",),
                                vmem_limit_bytes=V7X_VMEM_LIMIT_BYTES)


def _gmlp_mix(x, an_ref, win_ref, vg_ref, wsp_ref, bsp_ref, wout_ref):
    t_idx = lax.broadcasted_iota(jnp.int32, (CHUNK, CHUNK), 0)
    s_idx = lax.broadcasted_iota(jnp.int32, (CHUNK, CHUNK), 1)
    causal = s_idx <= t_idx
    n_chunks = x.shape[0] // CHUNK

    h =(x * _rms_scale(x) * an_ref[...]).astype(BF16)
    uv = _gelu_tanh(jnp.dot(h, win_ref[...], preferred_element_type=F32))
    u = uv[:, :D_MODEL]
    v = uv[:, D_MODEL:]
    mu = jnp.mean(v, axis=-1, keepdims=True)
    vc = v - mu
    var = jnp.mean(vc * vc, axis=-1, keepdims=True)
    vn = (vc * lax.rsqrt(var + EPS) * vg_ref[...]).astype(BF16)
    cols = []
    for g in range(N_GROUPS):
        wm = jnp.where(causal, wsp_ref[g], 0.0).astype(BF16)
        bias = bsp_ref[:, g * GROUP_DIM:(g + 1) * GROUP_DIM]
        vg = jnp.concatenate([vn[c * CHUNK:(c + 1) * CHUNK, g * GROUP_DIM:(g + 1) * GROUP_DIM]
                              for c in range(n_chunks)], axis=1)
        zg = jnp.dot(wm, vg, preferred_element_type=F32)
        cols.append(jnp.concatenate([zg[:, c * GROUP_DIM:(c + 1) * GROUP_DIM] + bias
                                     for c in range(n_chunks)], axis=0))
    z = jnp.concatenate(cols, axis=1)
    gated = (u * z).astype(BF16)
    return x + jnp.dot(gated, wout_ref[...], preferred_element_type=F32)


def _swiglu(x, fn, wgu_ref, wd_ref):
    h = (x * _rms_scale(x) * fn).astype(BF16)
    gu = jnp.dot(h, wgu_ref[...], preferred_element_type=F32)
    a = (_silu(gu[:, :D_FF]) * gu[:, D_FF:]).astype(BF16)
    return x + jnp.dot(a, wd_ref[...], preferred_element_type=F32)


def _ffn_qkv_kernel(x_ref, an_ref, win_ref, vg_ref, wsp_ref, bsp_ref, wout_ref,
                    fn_ref, wgu_ref, wd_ref, kvn_ref, wk_ref, wvt_ref, qn_ref, wqt_ref,
                    x_out, k_out, qt_out, vt_out, kn_out):
    x = _gmlp_mix(x_ref[...], an_ref, win_ref, vg_ref, wsp_ref, bsp_ref, wout_ref)
    y = _swiglu(x, fn_ref[...], wgu_ref, wd_ref)
    x_out[...] = y
    yn = y * _rms_scale(y)
    hk = (yn * kvn_ref[...]).astype(BF16)
    hq = (yn * qn_ref[...]).astype(BF16)
    kb = jnp.dot(hk, wk_ref[...], preferred_element_type=F32).astype(BF16)
    k_out[...] = kb
    nt = (((1,), (1,)), ((), ()))
    row = lax.broadcasted_iota(jnp.int32, (N_HEADS * KN_ROWS, D_MODEL), 0)
    col = lax.broadcasted_iota(jnp.int32, (N_HEADS * KN_ROWS, D_MODEL), 1)
    half = row % KN_ROWS
    pick = (half < 2) & (col // HEAD_DIM == 2 * (row // KN_ROWS) + half)
    sq = lax.dot_general(jnp.where(pick, 1.0, 0.0).astype(BF16), kb * kb, nt,
                         preferred_element_type=F32)
    kn_out[0] = sq.reshape(N_HEADS, KN_ROWS, sq.shape[1])
    vt_out[0] = lax.dot_general(wvt_ref[...], hk, nt, preferred_element_type=F32).astype(BF16)
    qt = lax.dot_general(wqt_ref[...], hq, nt, preferred_element_type=F32)
    qt_out[0] = (qt * (HEAD_DIM ** -0.5 * LOG2E)).astype(BF16)


def _ffn_qkv_layer(x, batch, gmlp_args, fn, w_gu, w_down, kv_norm, w_k, w_vt, q_norm, w_qt):
    n = x.shape[0]
    seq = n // batch
    tiles_per_seq = seq // ROW_TILE
    t_spec = pl.BlockSpec((1, D_MODEL, ROW_TILE),
                          lambda i: (i // tiles_per_seq, 0, i % tiles_per_seq))
    t_shape = jax.ShapeDtypeStruct((batch, D_MODEL, seq), BF16)
    return pl.pallas_call(
        _ffn_qkv_kernel,
        grid=(n // ROW_TILE,),
        in_specs=[
            _row_spec(D_MODEL),
            _const_spec((1, D_MODEL)),
            _const_spec((D_MODEL, 2 * D_MODEL)),
            _const_spec((1, D_MODEL)),
            _const_spec((N_GROUPS, CHUNK, CHUNK)),
            _const_spec((CHUNK, D_MODEL)),
            _const_spec((D_MODEL, D_MODEL)),
            _const_spec((1, D_MODEL)),
            _const_spec((D_MODEL, 2 * D_FF)),
            _const_spec((D_FF, D_MODEL)),
            _const_spec((1, D_MODEL)),
            _const_spec((D_MODEL, D_MODEL)),
            _const_spec((D_MODEL, D_MODEL)),
            _const_spec((1, D_MODEL)),
            _const_spec((D_MODEL, D_MODEL)),
        ],
        out_specs=[_row_spec(D_MODEL), _row_spec(D_MODEL), t_spec, t_spec,
                   pl.BlockSpec((1, N_HEADS, KN_ROWS, ROW_TILE),
                                lambda i: (i // tiles_per_seq, 0, 0, i % tiles_per_seq))],
        out_shape=[jax.ShapeDtypeStruct((n, D_MODEL), F32),
                   jax.ShapeDtypeStruct((n, D_MODEL), BF16), t_shape, t_shape,
                   jax.ShapeDtypeStruct((batch, N_HEADS, KN_ROWS, seq), F32)],
        compiler_params=_dense_params(),
        name="layer0_qkv",
    )(x, *gmlp_args, fn, w_gu, w_down, kv_norm, w_k, w_vt, q_norm, w_qt)


def _oproj_ffn_final_kernel(x_ref, a_ref, wo_ref, fn_ref, wgu_ref, wd_ref, gn_ref, o_ref):
    x = x_ref[...] + lax.dot_general(a_ref[0], wo_ref[...], (((0,), (0,)), ((), ())),
                                     preferred_element_type=F32)
    y = _swiglu(x, fn_ref[...], wgu_ref, wd_ref)
    o_ref[...] = y * _rms_scale(y) * gn_ref[...]


def _oproj_ffn_final_layer(x, attn_t, w_o, fn, w_gu, w_down, final_norm):
    n = x.shape[0]
    tiles_per_seq = attn_t.shape[2] // ROW_TILE
    return pl.pallas_call(
        _oproj_ffn_final_kernel,
        grid=(n // ROW_TILE,),
        in_specs=[
            _row_spec(D_MODEL),
            pl.BlockSpec((1, D_MODEL, ROW_TILE),
                         lambda i: (i // tiles_per_seq, 0, i % tiles_per_seq)),
            _const_spec((D_MODEL, D_MODEL)),
            _const_spec((1, D_MODEL)),
            _const_spec((D_MODEL, 2 * D_FF)),
            _const_spec((D_FF, D_MODEL)),
            _const_spec((1, D_MODEL)),
        ],
        out_specs=_row_spec(D_MODEL),
        out_shape=jax.ShapeDtypeStruct((n, D_MODEL), F32),
        compiler_params=_dense_params(),
        name="oproj_ffn1_final",
    )(x, attn_t, w_o, fn, w_gu, w_down, final_norm)


def _attn_kernel(coef_ref, lam_ref, qall_ref, k_ref, vt_ref, kn_ref, sub_ref, o_ref,
                 kext_ref, qext_ref, mask_ref, kmax2_ref, lamrow_ref, s_ref, acc_ref):
    head = pl.program_id(1)
    streams = range(BATCH_BLOCK)
    c_parts = [coef_ref[head, i] for i in range(3)]
    c = coef_ref[head, 3]

    lane = lax.broadcasted_iota(jnp.int32, (K_TILE, 128), 1)
    key_pos = lax.broadcasted_iota(jnp.int32, (K_TILE, 128), 0).astype(F32)
    kext = jnp.where(lane < 3, key_pos, 0.0)
    row = lax.broadcasted_iota(jnp.int32, (128, 2 * Q_TILE), 0)
    qry = lax.broadcasted_iota(jnp.int32, (128, 2 * Q_TILE), 1) % Q_TILE
    q_lo = -(qry % BF16_EXACT_INT).astype(F32)
    q_hi = -(qry - qry % BF16_EXACT_INT).astype(F32)
    qext = jnp.where((row >= 3) & (row < 6), q_lo, jnp.where((row >= 6) & (row < 9), q_hi, 0.0))
    for i in range(3):
        kext = jnp.where((lane == 3 + i) | (lane == 6 + i), c_parts[i], kext)
        qext = jnp.where(row == i, c_parts[i], qext)
    kext_ref[...] = kext.astype(BF16)
    qext_ref[...] = qext.astype(BF16)

    @pl.when((head == 0) & (pl.program_id(0) == 0))
    def _():
        key = lax.broadcasted_iota(jnp.int32, (K_TILE, Q_TILE), 0)
        qcol = lax.broadcasted_iota(jnp.int32, (K_TILE, Q_TILE), 1)
        for d in range(DIAG_TILES):
            hide = jnp.where(key + d * K_TILE <= qcol, 0.0, -jnp.inf)
            mask_ref[d] = jnp.concatenate([hide, hide], axis=1)

    lv = lam_ref[...]
    lam = (jnp.exp(jnp.sum(lv[0:1] * lv[1:2], axis=-1, keepdims=True))
           - jnp.exp(jnp.sum(lv[2:3] * lv[3:4], axis=-1, keepdims=True)) + LAMBDA_INIT)
    lamrow_ref[...] = jnp.broadcast_to(lam, (1, Q_TILE))

    kmax2 = []
    for st in streams:
        top_rows = jnp.max(kn_ref[st, 0], axis=1, keepdims=True)
        kmax2.append((top_rows[0:1], top_rows[1:2]))
        kmax2_ref[st] = jnp.concatenate([jnp.broadcast_to(top_rows[0:1], (1, Q_TILE)),
                                         jnp.broadcast_to(top_rows[1:2], (1, Q_TILE))], axis=1)

    def query_chunk(i, best):
        start = pl.multiple_of(i * NORM_CHUNK, NORM_CHUNK)
        for st in streams:
            q32 = qall_ref[st, :, pl.ds(start, NORM_CHUNK)].astype(F32)
            q32 = q32 * q32
            prod = jnp.concatenate(
                [jnp.sum(q32[:HEAD_DIM], axis=0, keepdims=True) * kmax2[st][0],
                 jnp.sum(q32[HEAD_DIM:], axis=0, keepdims=True) * kmax2[st][1]], axis=1)
            best = jnp.maximum(best, prod)
        return best

    top = lax.fori_loop(0, qall_ref.shape[2] // NORM_CHUNK, query_chunk,
                        jnp.zeros((1, 2 * NORM_CHUNK), F32))
    spread = 2.04 * jnp.sqrt(jnp.max(top, axis=1, keepdims=True))
    reach = (spread + SKIP_MARGIN) / c
    pairs = jnp.floor(reach * (1.0 / (2 * K_TILE))) + 1.0
    max_pairs = jnp.clip(pairs, 0.0, float(2 ** 20)).astype(jnp.int32)[0, 0]
    fixed_reference_ok = jnp.where(spread < FIXED_REFERENCE_SPREAD, 1, 0)[0, 0] == 1

    def query_tile(qi, carry):
        qi = jnp.asarray(qi, jnp.int32)
        q0 = pl.multiple_of(qi * Q_TILE, Q_TILE)

        def query_tile_rows(st):
            return qall_ref[st, :, pl.ds(q0, Q_TILE)]

        def query_matrix(st):
            qt = query_tile_rows(st)
            feat = lax.broadcasted_iota(jnp.int32, qt.shape, 0)
            zero = jnp.zeros_like(qt)
            return jnp.concatenate(
                [jnp.concatenate([jnp.where(feat < HEAD_DIM, qt, zero),
                                  jnp.where(feat >= HEAD_DIM, qt, zero)], axis=1),
                 qext_ref[...]], axis=0)

        def query_bounds(st):
            q32 = query_tile_rows(st).astype(F32)
            q32 = q32 * q32
            qn2 = jnp.concatenate([jnp.sum(q32[:HEAD_DIM], axis=0, keepdims=True),
                                   jnp.sum(q32[HEAD_DIM:], axis=0, keepdims=True)], axis=1)
            return 1.02 * jnp.sqrt(qn2 * kmax2_ref[st])

        n_pairs = jnp.minimum(qi, max_pairs)
        first = qi * DIAG_TILES

        def key_rows(st, kt):
            ks = pl.multiple_of(kt * K_TILE, K_TILE)
            return jnp.concatenate([k_ref[st, pl.ds(ks, K_TILE), :], kext_ref[...]], axis=1)

        def value_rows(st, kt):
            ks = pl.multiple_of(kt * K_TILE, K_TILE)
            return vt_ref[st, :, pl.ds(ks, K_TILE)]

        def tile_offset(kt):
            return c * (kt * K_TILE - qi * Q_TILE).astype(F32)

        @pl.when(fixed_reference_ok)
        def _():
            rhs = [query_matrix(st) for st in streams]
            bounds = [query_bounds(st) for st in streams]

            def tiles(t0, count, diagonal=False):
                kts, offs = [], []
                for i in range(count):
                    if diagonal and i >= DIAG_TILES:
                        kts.append(jnp.maximum(t0 - i, 0))
                        offs.append(jnp.where(n_pairs > 0, tile_offset(kts[i]), -jnp.inf))
                    else:
                        kts.append(t0 - i)
                        offs.append(tile_offset(kts[i]))
                los = [(DIAG_TILES - 1 - i) * K_TILE if diagonal and i < DIAG_TILES else 0
                       for i in range(count)]

                def narrow(a, lo):
                    return a if lo == 0 else jnp.concatenate([a[:, lo:Q_TILE], a[:, Q_TILE + lo:]], axis=1)

                def widen(a, lo):
                    if lo == 0:
                        return a
                    gap = jnp.zeros((a.shape[0], lo), a.dtype)
                    return jnp.concatenate([gap, a[:, :Q_TILE - lo], gap, a[:, Q_TILE - lo:]], axis=1)

                for st in streams:
                    probs, row_sums = [], None
                    for i in range(count):
                        s = jnp.dot(key_rows(st, kts[i]), narrow(rhs[st], los[i]),
                                    preferred_element_type=F32)
                        if diagonal and i < DIAG_TILES:
                            s = s + narrow(mask_ref[DIAG_TILES - 1 - i], los[i])
                        p = jnp.exp2(s + (narrow(bounds[st], los[i]) + offs[i]))
                        folded = jnp.sum(p.reshape(K_TILE // SUM_PARTIALS, SUM_PARTIALS, p.shape[1]), axis=0)
                        folded = widen(folded, los[i])
                        row_sums = folded if row_sums is None else row_sums + folded
                        probs.append(p.astype(BF16))
                    part = None
                    whole = [i for i in range(count) if los[i] == 0]
                    for g in range(0, len(whole), PV_TILES):
                        group = whole[g:g + PV_TILES]
                        values = jnp.concatenate([value_rows(st, kts[i]) for i in group], axis=1)
                        prod = jnp.dot(values, jnp.concatenate([probs[i] for i in group], axis=0),
                                       preferred_element_type=F32)
                        part = prod if part is None else part + prod
                    for i in range(count):
                        if los[i] > 0:
                            prod = jnp.dot(value_rows(st, kts[i]), probs[i], preferred_element_type=F32)
                            part = part + widen(prod, los[i])
                    if diagonal:
                        acc_ref[st, :V_DIM] = part
                        acc_ref[st, V_DIM:V_DIM + SUM_PARTIALS] = row_sums
                    else:
                        acc_ref[st, :V_DIM] += part
                        acc_ref[st, V_DIM:V_DIM + SUM_PARTIALS] += row_sums

            tiles(first + DIAG_TILES - 1, DIAG_TILES + 2, diagonal=True)
            far_pairs = jnp.maximum(n_pairs - 1, 0)
            n_quads = far_pairs // 2
            far_start = first - 3

            def quad(j, carry):
                tiles(far_start - 4 * j, 4)
                return carry

            lax.fori_loop(0, n_quads, quad, 0)

            @pl.when(far_pairs % 2 == 1)
            def _():
                tiles(far_start - 4 * n_quads, 2)

        @pl.when(jnp.logical_not(fixed_reference_ok))
        def _():
            rhs = [query_matrix(st) for st in streams]
            acc_ref[...] = jnp.zeros_like(acc_ref)
            sum_rows = jnp.where(lax.broadcasted_iota(jnp.int32, (BF16_SUBLANES, K_TILE), 0) == 0,
                                 1.0, 0.0).astype(BF16)

            def scores(st, kt, buf, diag=None):
                s = jnp.dot(key_rows(st, kt), rhs[st], preferred_element_type=F32)
                if diag is not None:
                    s = s + mask_ref[diag]
                s_ref[st, buf] = s
                return jnp.max(s, axis=0, keepdims=True)

            def update(st, kt, buf, smax, m):
                off = tile_offset(kt)
                m_new = jnp.maximum(m, smax + off)
                alpha = jnp.exp2(m - m_new)
                p = jnp.exp2(s_ref[st, buf] - (m_new - off))
                lhs = jnp.concatenate([value_rows(st, kt), sum_rows], axis=0)
                acc_ref[st] = alpha * acc_ref[st] + jnp.dot(lhs, p.astype(BF16),
                                                            preferred_element_type=F32)
                return m_new

            m = [jnp.full((1, 2 * Q_TILE), -1e30, F32) for _ in streams]
            smax_b = [scores(st, first + 1, 1, diag=1) for st in streams]
            smax_a = [scores(st, first, 0, diag=0) for st in streams]
            m = [update(st, first + 1, 1, smax_b[st], m[st]) for st in streams]

            def pair(j, carry):
                smax_a, m = carry
                t = first - 2 * j
                smax_b = [scores(st, t - 1, 1) for st in streams]
                m = [update(st, t, 0, smax_a[st], m[st]) for st in streams]
                smax_a = [scores(st, t - 2, 0) for st in streams]
                m = [update(st, t - 1, 1, smax_b[st], m[st]) for st in streams]
                return smax_a, m

            smax_a, m = lax.fori_loop(0, n_pairs, pair, (smax_a, m))
            for st in streams:
                update(st, first - 2 * n_pairs, 0, smax_a[st], m[st])

        lam = lamrow_ref[...]
        for st in streams:
            acc = acc_ref[st, :V_DIM]
            sums = jnp.sum(acc_ref[st, V_DIM:V_DIM + SUM_PARTIALS], axis=0, keepdims=True)
            inv = 1.0 / sums
            o = (acc[:, :Q_TILE] * inv[:, :Q_TILE]
                 - acc[:, Q_TILE:] * (lam * inv[:, Q_TILE:]))
            norm = lax.rsqrt(jnp.mean(o * o, axis=0, keepdims=True) + EPS) * (1.0 - LAMBDA_INIT)
            o_ref[st, :, pl.ds(q0, Q_TILE)] = (o * norm * sub_ref[...]).astype(BF16)
        return carry

    lax.fori_loop(0, qall_ref.shape[2] // Q_TILE, query_tile, 0)


def _alibi_coefficients():
    pieces = []
    rest = np.float64(LOG2E)
    for _ in range(3):
        piece = np.float64(np.asarray(rest, np.float32).astype(jnp.bfloat16).astype(np.float32))
        pieces.append(piece)
        rest = rest - piece
    rows = []
    for h in range(N_HEADS):
        slope = 2.0 ** (-8.0 * (h + 1) / N_HEADS)
        rows.append([slope * p for p in pieces] + [slope * sum(pieces)])
    return jnp.asarray(np.array(rows, dtype=np.float32))


def _diff_attention(lam_vecs, qt, k, vt, kn, subln):
    b, s, _ = k.shape
    return pl.pallas_call(
        _attn_kernel,
        grid=(b // BATCH_BLOCK, N_HEADS),
        in_specs=[
            pl.BlockSpec(memory_space=pltpu.SMEM),
            pl.BlockSpec((4, HEAD_DIM), lambda bi, hi: (0, 0)),
            pl.BlockSpec((BATCH_BLOCK, V_DIM, s), lambda bi, hi: (bi, hi, 0)),
            pl.BlockSpec((BATCH_BLOCK, s, V_DIM), lambda bi, hi: (bi, 0, hi)),
            pl.BlockSpec((BATCH_BLOCK, V_DIM, s), lambda bi, hi: (bi, hi, 0)),
            pl.BlockSpec((BATCH_BLOCK, 1, KN_ROWS, s), lambda bi, hi: (bi, hi, 0, 0)),
            pl.BlockSpec((V_DIM, Q_TILE), lambda bi, hi: (0, 0)),
        ],
        out_specs=pl.BlockSpec((BATCH_BLOCK, V_DIM, s), lambda bi, hi: (bi, hi, 0)),
        out_shape=jax.ShapeDtypeStruct((b, N_HEADS * V_DIM, s), BF16),
        scratch_shapes=[
            pltpu.VMEM((K_TILE, 128), BF16),
            pltpu.VMEM((128, 2 * Q_TILE), BF16),
            pltpu.VMEM((DIAG_TILES, K_TILE, 2 * Q_TILE), F32),
            pltpu.VMEM((BATCH_BLOCK, 1, 2 * Q_TILE), F32),
            pltpu.VMEM((1, Q_TILE), F32),
            pltpu.VMEM((BATCH_BLOCK, 2, K_TILE, 2 * Q_TILE), F32),
            pltpu.VMEM((BATCH_BLOCK, V_DIM + BF16_SUBLANES, 2 * Q_TILE), F32),
        ],
        compiler_params=pltpu.CompilerParams(
            dimension_semantics=("arbitrary", "arbitrary"),
            vmem_limit_bytes=V7X_VMEM_LIMIT_BYTES),
        name="diff_attention",
    )(_alibi_coefficients(), lam_vecs, qt, k, vt, kn, subln)


def kernel(x, a_norm, a_w_in, a_v_norm, a_w_sp, a_b_sp, a_w_out, ffn_norm, ffn_w_gu, ffn_w_down,
           kv_norm, kv_w, b_norm, b_w_q, b_lambda, b_subln, b_w_o, final_norm):
    b, s, d = x.shape
    assert d == D_MODEL and s % Q_TILE == 0 and s % ROW_TILE == 0
    assert DIAG_TILES == 2 and K_TILE <= BF16_EXACT_INT and b % BATCH_BLOCK == 0
    xs = x.reshape(b * s, d)
    row = lambda g: g.reshape(1, -1).astype(F32)

    b_full = jnp.repeat(a_b_sp[0].T.astype(F32), GROUP_DIM, axis=1)
    gmlp_args = (row(a_norm[0]), a_w_in[0].astype(BF16), row(a_v_norm[0]),
                 a_w_sp[0].astype(F32), b_full, a_w_out[0].astype(BF16))

    w_k = kv_w[:, :D_MODEL].astype(BF16)
    w_vt = kv_w[:, D_MODEL:].T.astype(BF16)
    w_qt = b_w_q[0].T.astype(BF16)
    xs, k, qt, vt, kn = _ffn_qkv_layer(xs, b, gmlp_args, row(ffn_norm[0]), ffn_w_gu[0].astype(BF16),
                                       ffn_w_down[0].astype(BF16), row(kv_norm), w_k, w_vt,
                                       row(b_norm[0]), w_qt)

    sub_cols = jnp.broadcast_to(b_subln[0].astype(F32)[:, None], (V_DIM, Q_TILE))
    attn = _diff_attention(b_lambda[0].astype(F32), qt, k.reshape(b, s, d), vt, kn, sub_cols)

    out = _oproj_ffn_final_layer(xs, attn, b_w_o[0].astype(BF16),
                                 row(ffn_norm[1]), ffn_w_gu[1].astype(BF16),
                                 ffn_w_down[1].astype(BF16), row(final_norm))
    return out.reshape(b, s, d)
```
